```python
import math
import jax, jax.numpy as jnp
from jax import lax
import numpy as np

D_MODEL = 1024
BATCH = 16
SEQ = 2048
DEPTH = 2

GRID_W = 64
CTX_LEN = 256
N_EVEN = (DEPTH + 1) // 2
N_ODD = DEPTH // 2
EPS = 1e-6

ROPE_HEAD_DIM = 64
ROPE_FREQS = ROPE_HEAD_DIM // 4
ROPE_BASE = 10000.0

SC_WIDTH = D_MODEL // 2
DA_HEADS = 4
DA_HEAD_DIM = ROPE_HEAD_DIM
DA_V_DIM = 2 * DA_HEAD_DIM
DA_QK_WIDTH = DA_HEADS * 2 * DA_HEAD_DIM
DA_WIDTH = DA_HEADS * DA_V_DIM
Q_BLOCK = 128
EVEN_SPLITS = (SC_WIDTH, 2 * SC_WIDTH, 3 * SC_WIDTH, 3 * SC_WIDTH + DA_QK_WIDTH, 3 * SC_WIDTH + 2 * DA_QK_WIDTH)
EVEN_IN = 3 * SC_WIDTH + 2 * DA_QK_WIDTH + DA_WIDTH
EVEN_OUT = SC_WIDTH + DA_WIDTH

POOL_WINDOWS = (2, 4, 8, 16)
POOL_RADII = tuple(w // 2 for w in POOL_WINDOWS)
POOL_GROUPS = 4
POOL_WIDTH = D_MODEL // 2
POOL_GROUP_DIM = POOL_WIDTH // POOL_GROUPS
RET_HEADS = 4
RET_QK_DIM = ROPE_HEAD_DIM
RET_V_DIM = 2 * RET_QK_DIM
RET_QK_WIDTH = RET_HEADS * RET_QK_DIM
RET_WIDTH = RET_HEADS * RET_V_DIM
RET_CHUNK = 128
ODD_SPLITS = (POOL_WIDTH, POOL_WIDTH + RET_QK_WIDTH, POOL_WIDTH + 2 * RET_QK_WIDTH, POOL_WIDTH + 2 * RET_QK_WIDTH + RET_WIDTH)
ODD_IN = POOL_WIDTH + 2 * RET_QK_WIDTH + 2 * RET_WIDTH
ODD_OUT = POOL_WIDTH + RET_WIDTH

FFN_DIM = 2816

kernel_name = 'hybrid_conv_diffattn_pool_retention_dit'


def rms_norm(x, g):
    xf = x.astype(jnp.float32)
    y = xf * lax.rsqrt(jnp.mean(xf * xf, axis=-1, keepdims=True) + EPS)
    return (y * g.astype(jnp.float32)).astype(x.dtype)


def modulate(h, shift, scale):
    return h * (1 + scale) + shift


def dwconv3(x, w):
    xp = jnp.pad(x, ((0, 0), (1, 1), (0, 0)))
    return xp[:, :-2] * w[0] + xp[:, 1:-1] * w[1] + xp[:, 2:] * w[2]


def rope_tables(rows):
    row = jnp.repeat(jnp.arange(rows, dtype=jnp.float32), GRID_W)
    col = jnp.tile(jnp.arange(GRID_W, dtype=jnp.float32), rows)
    inv = ROPE_BASE ** (-jnp.arange(ROPE_FREQS, dtype=jnp.float32) / ROPE_FREQS)
    ang = jnp.stack([row[:, None] * inv, col[:, None] * inv], axis=1)
    return jnp.cos(ang), jnp.sin(ang)


def apply_rope(x, cos, sin):
    shp = x.shape
    xr = x.reshape(*shp[:-1], 2, 2, ROPE_FREQS)
    bc = (shp[1],) + (1,) * (x.ndim - 3) + (2, ROPE_FREQS)
    c = cos.reshape(bc).astype(x.dtype)
    s = sin.reshape(bc).astype(x.dtype)
    x1, x2 = xr[..., 0, :], xr[..., 1, :]
    out = jnp.stack([x1 * c - x2 * s, x1 * s + x2 * c], axis=-2)
    return out.reshape(shp)


def conv_ffn(h, w_up, conv_w, conv_b, w_down):
    a, b = jnp.split(h @ w_up, 2, axis=-1)
    a = dwconv3(a, conv_w) + conv_b
    return (jax.nn.silu(a) * b) @ w_down


def diff_lambda_init(layer_idx):
    return 0.8 - 0.6 * math.exp(-0.3 * layer_idx)


def diff_attend(q, k, v, lam):
    s = jnp.einsum('bhmqd,bhmkd->bhmqk', q, k).astype(jnp.float32) * (DA_HEAD_DIM ** -0.5)
    p = jax.nn.softmax(s, axis=-1)
    p = p[:, :, 0] - lam * p[:, :, 1]
    return jnp.einsum('bhqk,bhkv->bhqv', p.astype(v.dtype), v)


def conv_diff_mixer(h_ctx, h_lat, w_in, w_out, sc_conv_w, q_g, k_g, lq1, lk1, lq2, lk2, subln_g, lam_init, cos, sin):
    f32 = jnp.float32
    lam = (jnp.exp(jnp.sum(lq1.astype(f32) * lk1.astype(f32)))
           - jnp.exp(jnp.sum(lq2.astype(f32) * lk2.astype(f32))) + lam_init)

    def project(h, rotate):
        b_g, c_g, xv, q, k, v = jnp.split(h @ w_in, EVEN_SPLITS, axis=-1)
        B_, N = h.shape[:2]
        q = rms_norm(q.reshape(B_, N, DA_HEADS, 2, DA_HEAD_DIM), q_g)
        k = rms_norm(k.reshape(B_, N, DA_HEADS, 2, DA_HEAD_DIM), k_g)
        if rotate:
            q = apply_rope(q, cos, sin)
            k = apply_rope(k, cos, sin)
        q = q.transpose(0, 2, 3, 1, 4)
        k = k.transpose(0, 2, 3, 1, 4)
        v = v.reshape(B_, N, DA_HEADS, DA_V_DIM).transpose(0, 2, 1, 3)
        return (b_g, c_g, xv), q, k, v

    def finish(gates, o):
        b_g, c_g, xv = gates
        y_conv = b_g * dwconv3(c_g * xv, sc_conv_w)
        B_, _, N, _ = o.shape
        o = rms_norm(o.transpose(0, 2, 1, 3), subln_g) * (1.0 - lam_init)
        return jnp.concatenate([y_conv, o.reshape(B_, N, DA_WIDTH).astype(y_conv.dtype)], axis=-1) @ w_out

    gates_c, qc, kc, vc = project(h_ctx, False)
    gates_l, ql, kl, vl = project(h_lat, True)
    k_all = jnp.concatenate([kl, kc], axis=3)
    v_all = jnp.concatenate([vl, vc], axis=2)
    B_, H, _, N, d = ql.shape
    nb = N // Q_BLOCK
    q_blocks = jnp.moveaxis(ql.reshape(B_, H, 2, nb, Q_BLOCK, d), 3, 0)
    o_blocks = lax.map(lambda qb: diff_attend(qb, k_all, v_all, lam), q_blocks)
    o_lat = jnp.moveaxis(o_blocks, 0, 2).reshape(B_, H, N, DA_V_DIM)
    y_lat = finish(gates_l, o_lat)
    y_ctx = finish(gates_c, diff_attend(qc, kc, vc, lam))
    return y_ctx, y_lat


def multiscale_pool(v, pool_w, pool_scale):
    B_, N, _ = v.shape
    vg = v.reshape(B_, N, POOL_GROUPS, POOL_GROUP_DIM).astype(jnp.float32)
    cs = jnp.concatenate([jnp.zeros_like(vg[:, :1]), jnp.cumsum(vg, axis=1)], axis=1)
    t = jnp.arange(N)[:, None]
    r = jnp.array(POOL_RADII)[None, :]
    lo = jnp.clip(t - r, 0, N)
    hi = jnp.clip(t + r + 1, 0, N)
    g = jnp.arange(POOL_GROUPS)[None, :]
    mean = (cs[:, hi, g] - cs[:, lo, g]) / (hi - lo).astype(jnp.float32)[:, :, None]
    y = (mean - vg).astype(v.dtype)
    y = jnp.einsum('bngc,gcd->bngd', y, pool_w)
    return y.reshape(B_, N, POOL_WIDTH) * pool_scale


def retention_scan(q, k, v, lg, s0):
    B_, H, N, _ = q.shape
    dv = v.shape[-1]
    nc = N // RET_CHUNK

    def chunks(t):
        return jnp.moveaxis(t.reshape(B_, H, nc, RET_CHUNK, t.shape[-1]), 2, 0)

    pos = jnp.arange(RET_CHUNK, dtype=jnp.float32)
    diff = pos[:, None] - pos[None, :]
    intra = jnp.where(diff >= 0, jnp.exp(lg[:, None, None] * jnp.maximum(diff, 0.0)), 0.0)
    q_dec = jnp.exp(lg[:, None] * (pos + 1.0))[None, :, :, None]
    k_dec = jnp.exp(lg[:, None] * (RET_CHUNK - 1.0 - pos))[None, :, :, None]
    c_dec = jnp.exp(lg * RET_CHUNK)[None, :, None, None]

    def step(s, qkv):
        qc, kc, vc = qkv
        att = jnp.einsum('bhid,bhjd->bhij', qc, kc) * intra
        o = jnp.einsum('bhij,bhjv->bhiv', att, vc) + jnp.einsum('bhid,bhdv->bhiv', qc * q_dec, s)
        s = s * c_dec + jnp.einsum('bhjd,bhjv->bhdv', kc * k_dec, vc)
        return s, o

    _, o = lax.scan(step, s0, (chunks(q), chunks(k), chunks(v)))
    return jnp.moveaxis(o, 0, 2).reshape(B_, H, N, dv)


def retention_bidir(q, k, v, lg_f, lg_b, s_f, s_b):
    o_f = retention_scan(q, k, v, lg_f, s_f)
    flip = lambda t: jnp.flip(t, axis=2)
    o_b = flip(retention_scan(flip(q), flip(k), flip(v), lg_b, s_b))
    return o_f + o_b


def context_states(k, v, lg_f, lg_b):
    L = k.shape[2]
    j = jnp.arange(L, dtype=jnp.float32)
    w_f = jnp.exp(lg_f[:, None] * (L - 1.0 - j))
    w_b = jnp.exp(lg_b[:, None] * j)
    s_f = jnp.einsum('bhjd,hj,bhjv->bhdv', k, w_f, v)
    s_b = jnp.einsum('bhjd,hj,bhjv->bhdv', k, w_b, v)
    return s_f, s_b


def pool_retention_mixer(h_ctx, h_lat, w_in, w_out, pool_w, pool_scale, dec_f, dec_b, gn_g, cos, sin, with_ctx):
    f32 = jnp.float32
    lg_f = -jnp.exp(dec_f.astype(f32))
    lg_b = -jnp.exp(dec_b.astype(f32))

    def heads(t, dh):
        B_, N = t.shape[:2]
        return t.reshape(B_, N, RET_HEADS, dh)

    def qkv_heads(q, k, v, rotate):
        q, k, v = heads(q, RET_QK_DIM), heads(k, RET_QK_DIM), heads(v, RET_V_DIM)
        if rotate:
            q = apply_rope(q, cos, sin)
            k = apply_rope(k, cos, sin)
        q = (q.astype(f32) * (RET_QK_DIM ** -0.5)).transpose(0, 2, 1, 3)
        k = k.astype(f32).transpose(0, 2, 1, 3)
        v = v.astype(f32).transpose(0, 2, 1, 3)
        return q, k, v

    def finish(pv, g, o):
        B_, _, N, _ = o.shape
        y_r = rms_norm(o.transpose(0, 2, 1, 3), gn_g).reshape(B_, N, RET_WIDTH).astype(g.dtype) * jax.nn.silu(g)
        y_p = multiscale_pool(pv, pool_w, pool_scale)
        return jnp.concatenate([y_p, y_r], axis=-1) @ w_out

    pv_l, q_l, k_l, v_l, g_l = jnp.split(h_lat @ w_in, ODD_SPLITS, axis=-1)
    ql, kl, vl = qkv_heads(q_l, k_l, v_l, True)
    if with_ctx:
        pv_c, q_c, k_c, v_c, g_c = jnp.split(h_ctx @ w_in, ODD_SPLITS, axis=-1)
        qc, kc, vc = qkv_heads(q_c, k_c, v_c, False)
    else:
        k_c, v_c = jnp.split(h_ctx @ w_in[:, ODD_SPLITS[1]:ODD_SPLITS[3]], [RET_QK_WIDTH], axis=-1)
        kc = heads(k_c, RET_QK_DIM).astype(f32).transpose(0, 2, 1, 3)
        vc = heads(v_c, RET_V_DIM).astype(f32).transpose(0, 2, 1, 3)
    s_f, s_b = context_states(kc, vc, lg_f, lg_b)
    y_lat = finish(pv_l, g_l, retention_bidir(ql, kl, vl, lg_f, lg_b, s_f, s_b))
    y_ctx = None
    if with_ctx:
        zero = jnp.zeros_like(s_f)
        y_ctx = finish(pv_c, g_c, retention_bidir(qc, kc, vc, lg_f, lg_b, zero, zero))
    return y_ctx, y_lat


def setup_inputs(seed: int = 0) -> dict:
    key = jax.random.key(seed)
    ks = iter(jax.random.split(key, 40))

    def nrm(shape, scale=1.0):
        return scale * jax.random.normal(next(ks), shape, jnp.float32)

    def gain(shape):
        return 1.0 + nrm(shape, 0.05)

    base_decay = jnp.log(-jnp.log1p(-(2.0 ** (-5.0 - jnp.arange(RET_HEADS, dtype=jnp.float32)))))
    return {
        'x': nrm((BATCH, SEQ, D_MODEL)),
        'c': nrm((BATCH, D_MODEL)),
        'ctx': nrm((BATCH, CTX_LEN, D_MODEL)),
        'c_ctx': nrm((D_MODEL,)),
        'mod_w': nrm((DEPTH, D_MODEL, 6 * D_MODEL), 0.5 * D_MODEL ** -0.5),
        'mod_b': nrm((DEPTH, 6 * D_MODEL), 0.01),
        'norm1_g': gain((DEPTH, D_MODEL)),
        'norm2_g': gain((DEPTH, D_MODEL)),
        'ev_w_in': nrm((N_EVEN, D_MODEL, EVEN_IN), D_MODEL ** -0.5),
        'ev_w_out': nrm((N_EVEN, EVEN_OUT, D_MODEL), EVEN_OUT ** -0.5),
        'sc_conv_w': nrm((N_EVEN, 3, SC_WIDTH), 3 ** -0.5),
        'da_q_norm': gain((N_EVEN, DA_HEAD_DIM)),
        'da_k_norm': gain((N_EVEN, DA_HEAD_DIM)),
        'da_lq1': nrm((N_EVEN, DA_HEAD_DIM), 0.1),
        'da_lk1': nrm((N_EVEN, DA_HEAD_DIM), 0.1),
        'da_lq2': nrm((N_EVEN, DA_HEAD_DIM), 0.1),
        'da_lk2': nrm((N_EVEN, DA_HEAD_DIM), 0.1),
        'da_subln_g': gain((N_EVEN, DA_V_DIM)),
        'od_w_in': nrm((N_ODD, D_MODEL, ODD_IN), D_MODEL ** -0.5),
        'od_w_out': nrm((N_ODD, ODD_OUT, D_MODEL), ODD_OUT ** -0.5),
        'pool_w': nrm((N_ODD, POOL_GROUPS, POOL_GROUP_DIM, POOL_GROUP_DIM), POOL_GROUP_DIM ** -0.5),
        'pool_scale': gain((N_ODD, POOL_WIDTH)),
        'ret_decay_f': base_decay + nrm((N_ODD, RET_HEADS), 0.1),
        'ret_decay_b': base_decay + nrm((N_ODD, RET_HEADS), 0.1),
        'ret_gn_g': gain((N_ODD, RET_V_DIM)),
        'ffn_w_up': nrm((DEPTH, D_MODEL, 2 * FFN_DIM), D_MODEL ** -0.5),
        'ffn_conv_w': nrm((DEPTH, 3, FFN_DIM), 3 ** -0.5),
        'ffn_conv_b': nrm((DEPTH, FFN_DIM), 0.01),
        'ffn_w_down': nrm((DEPTH, FFN_DIM, D_MODEL), FFN_DIM ** -0.5),
    }


def reference(x, c, ctx, c_ctx, mod_w, mod_b, norm1_g, norm2_g, ev_w_in, ev_w_out, sc_conv_w, da_q_norm, da_k_norm,
              da_lq1, da_lk1, da_lq2, da_lk2, da_subln_g, od_w_in, od_w_out, pool_w, pool_scale, ret_decay_f,
              ret_decay_b, ret_gn_g, ffn_w_up, ffn_conv_w, ffn_conv_b, ffn_w_down):
    ROWS = x.shape[1] // GRID_W
    cos, sin = rope_tables(ROWS)
    silu_c = jax.nn.silu(c)
    silu_cc = jax.nn.silu(c_ctx)
    for i in range(DEPTH):
        last = i == DEPTH - 1
        j = i // 2
        mod_l = (silu_c @ mod_w[i] + mod_b[i])[:, None, :]
        mod_c = silu_cc @ mod_w[i] + mod_b[i]
        sh1, sc1, g1, sh2, sc2, g2 = jnp.split(mod_l, 6, axis=-1)
        sh1c, sc1c, g1c, sh2c, sc2c, g2c = jnp.split(mod_c, 6, axis=-1)
        h_lat = modulate(rms_norm(x, norm1_g[i]), sh1, sc1)
        h_ctx = modulate(rms_norm(ctx, norm1_g[i]), sh1c, sc1c)
        if i % 2 == 0:
            y_ctx, y_lat = conv_diff_mixer(h_ctx, h_lat, ev_w_in[j], ev_w_out[j], sc_conv_w[j], da_q_norm[j],
                                           da_k_norm[j], da_lq1[j], da_lk1[j], da_lq2[j], da_lk2[j],
                                           da_subln_g[j], diff_lambda_init(i), cos, sin)
        else:
            y_ctx, y_lat = pool_retention_mixer(h_ctx, h_lat, od_w_in[j], od_w_out[j], pool_w[j], pool_scale[j],
                                                ret_decay_f[j], ret_decay_b[j], ret_gn_g[j], cos, sin, not last)
        x = x + g1 * y_lat
        x = x + g2 * conv_ffn(modulate(rms_norm(x, norm2_g[i]), sh2, sc2),
                              ffn_w_up[i], ffn_conv_w[i], ffn_conv_b[i], ffn_w_down[i])
        if not last:
            ctx = ctx + g1c * y_ctx
            ctx = ctx + g2c * conv_ffn(modulate(rms_norm(ctx, norm2_g[i]), sh2c, sc2c),
                                       ffn_w_up[i], ffn_conv_w[i], ffn_conv_b[i], ffn_w_down[i])
    return x
```

```python
import functools
import math

import jax
import jax.numpy as jnp
from jax import lax
from jax.experimental import pallas as pl
from jax.experimental.pallas import tpu as pltpu

F32 = jnp.float32
BF16 = jnp.bfloat16

EPS = 1e-6
GRID_W = 64
ROPE_HEAD_DIM = 64
ROPE_FREQS = ROPE_HEAD_DIM // 4
ROPE_BASE = 10000.0
HEADS = 4
HEAD_V = 128
RET_CHUNK = 128
POOL_RADII = (1, 2, 4, 8)
LANES = 128
HALO = 16
VMEM_LIMIT = 56 * 1024 * 1024


def _params(sem, vmem=VMEM_LIMIT):
    return pltpu.CompilerParams(dimension_semantics=sem, vmem_limit_bytes=vmem)


def _sigmoid(x):
    return 1.0 / (1.0 + jnp.exp(-x))


def _dot(a, b):
    return jnp.dot(a, b, preferred_element_type=F32)


def _dot_nt(a, b):
    return lax.dot_general(a, b, (((1,), (1,)), ((), ())), preferred_element_type=F32)


def _dot_tn(a, b):
    return lax.dot_general(a, b, (((0,), (0,)), ((), ())), preferred_element_type=F32)


def _rms(x, width):
    return x * lax.rsqrt(jnp.sum(x * x, axis=-1, keepdims=True) * (1.0 / width) + EPS)


def _norm_mod(x, g, shift, scale):
    return (_rms(x, x.shape[-1]) * g) * (1.0 + scale) + shift


def _half_mask(shape):
    return lax.broadcasted_iota(jnp.int32, shape, len(shape) - 1) < (LANES // 2)


def _rope(t, cos, sin):
    lane = lax.broadcasted_iota(jnp.int32, t.shape, 1)
    first = lax.rem(lane, 2 * ROPE_FREQS) < ROPE_FREQS
    partner = jnp.where(first, pltpu.roll(t, LANES - ROPE_FREQS, axis=1), pltpu.roll(t, ROPE_FREQS, axis=1))
    return t * cos + partner * sin


def _norm64(t, gain):
    lo = _half_mask(t.shape)
    sq = t * t
    s_lo = jnp.sum(jnp.where(lo, sq, 0.0), axis=-1, keepdims=True)
    s_hi = jnp.sum(jnp.where(lo, 0.0, sq), axis=-1, keepdims=True)
    w = 1.0 / (LANES // 2)
    r = jnp.where(lo, lax.rsqrt(s_lo * w + EPS), lax.rsqrt(s_hi * w + EPS))
    return (t * r) * gain


def _mod_kernel(c_ref, w_ref, b_ref, o_ref):
    c = c_ref[...]
    a = c * _sigmoid(c)
    o_ref[...] = jnp.dot(a, w_ref[...], preferred_element_type=F32,
                         precision=lax.Precision.HIGHEST) + b_ref[...]


def _modulation(c_all, mod_w, mod_b):
    depth, d, n6 = mod_w.shape
    rows = c_all.shape[0]
    tn = 1536
    return pl.pallas_call(
        _mod_kernel,
        grid=(depth, n6 // tn),
        in_specs=[
            pl.BlockSpec((rows, d), lambda l, j: (0, 0)),
            pl.BlockSpec((None, d, tn), lambda l, j: (l, 0, j)),
            pl.BlockSpec((None, 1, tn), lambda l, j: (l, 0, j)),
        ],
        out_specs=pl.BlockSpec((None, rows, tn), lambda l, j: (l, 0, j)),
        out_shape=jax.ShapeDtypeStruct((depth, rows, n6), F32),
        compiler_params=_params(("parallel", "parallel")),
        name="modulation",
    )(c_all, mod_w, mod_b.reshape(depth, 1, n6))


def _in_proj_kernel(*refs, segs, rotate, n_gain):
    x_ref, mod_ref, g_ref, w_ref = refs[:4]
    gain_refs = refs[4:4 + n_gain]
    pos = 4 + n_gain
    if rotate:
        cos_ref, sin_ref = refs[pos:pos + 2]
        pos += 2
    out_refs = refs[pos:]
    h = _norm_mod(x_ref[...], g_ref[...], mod_ref[0:1, :], mod_ref[1:2, :]).astype(BF16)
    for (c0, c1, mode, gain_idx, scale), o_ref in zip(segs, out_refs):
        y = _dot(h, w_ref[:, c0:c1])
        if mode == "copy":
            o_ref[...] = y.astype(o_ref.dtype)
            continue
        for s in range((c1 - c0) // LANES):
            t = y[:, s * LANES:(s + 1) * LANES]
            if mode == "qknorm":
                t = _norm64(t, gain_refs[gain_idx][...])
            if rotate:
                t = _rope(t, cos_ref[...], sin_ref[...])
            if scale != 1.0:
                t = t * scale
            o_ref[:, s * LANES:(s + 1) * LANES] = t.astype(o_ref.dtype)


def _in_proj(x, mod, g, w, segs, out_dtypes, gains, rope, seq_len, tm, name):
    t_rows, d = x.shape
    per_seq = mod.shape[0] > 1
    tiles_per_seq = max(seq_len // tm, 1)
    if per_seq:
        assert seq_len % tm == 0
        mod_map = lambda i: (i // tiles_per_seq, 0, 0)
    else:
        mod_map = lambda i: (0, 0, 0)
    in_specs = [
        pl.BlockSpec((tm, d), lambda i: (i, 0)),
        pl.BlockSpec((None, 6, d), mod_map),
        pl.BlockSpec((1, d), lambda i: (0, 0)),
        pl.BlockSpec(w.shape, lambda i: (0, 0)),
    ]
    args = [x, mod, g.reshape(1, d), w]
    for gn in gains:
        in_specs.append(pl.BlockSpec((1, LANES), lambda i: (0, 0)))
        args.append(jnp.tile(gn.reshape(1, -1), (1, LANES // gn.shape[-1])))
    if rope is not None:
        assert seq_len % tm == 0
        for tab in rope:
            in_specs.append(pl.BlockSpec((tm, LANES), lambda i: (i % tiles_per_seq, 0)))
            args.append(tab)
    out_specs = [pl.BlockSpec((tm, c1 - c0), lambda i: (i, 0)) for (c0, c1, _, _, _) in segs]
    out_shape = [jax.ShapeDtypeStruct((t_rows, c1 - c0), dt) for (c0, c1, _, _, _), dt in zip(segs, out_dtypes)]
    return pl.pallas_call(
        functools.partial(_in_proj_kernel, segs=tuple(segs), rotate=rope is not None, n_gain=len(gains)),
        grid=(t_rows // tm,),
        in_specs=in_specs,
        out_specs=out_specs,
        out_shape=out_shape,
        compiler_params=_params(("parallel",)),
        name=name,
    )(*args)


def _diff_attn_kernel(*refs, n_src, lam_init):
    lam_ref, q_ref = refs[:2]
    kv_refs = refs[2:2 + 2 * n_src]
    o_ref = refs[2 + 2 * n_src]
    lv = lam_ref[...]
    lam = (jnp.exp(jnp.sum(lv[0:1] * lv[1:2], axis=-1, keepdims=True))
           - jnp.exp(jnp.sum(lv[2:3] * lv[3:4], axis=-1, keepdims=True)) + lam_init)
    q = q_ref[...]
    s0, s1, vs = [], [], []
    for j in range(n_src):
        k = kv_refs[2 * j][...]
        lo = _half_mask(k.shape)
        zero = jnp.zeros_like(k)
        s0.append(_dot_nt(q, jnp.where(lo, k, zero)))
        s1.append(_dot_nt(q, jnp.where(lo, zero, k)))
        vs.append(kv_refs[2 * j + 1][...])

    def softmax_parts(ss):
        m = functools.reduce(jnp.maximum, [jnp.max(s, axis=-1, keepdims=True) for s in ss])
        ps = [jnp.exp(s - m) for s in ss]
        l = functools.reduce(jnp.add, [jnp.sum(p, axis=-1, keepdims=True) for p in ps])
        return ps, l

    p0, l0 = softmax_parts(s0)
    p1, l1 = softmax_parts(s1)
    w0 = 1.0 / l0
    w1 = lam / l1
    o = None
    for j in range(n_src):
        p = (p0[j] * w0 - p1[j] * w1).astype(BF16)
        t = _dot(p, vs[j])
        o = t if o is None else o + t
    o_ref[...] = o.astype(o_ref.dtype)


def _diff_attn(lam_vecs, q, kvs, seq_q, tq, lam_init, name):
    t_rows, width = q.shape
    batch = t_rows // seq_q
    nq = seq_q // tq
    in_specs = [
        pl.BlockSpec(lam_vecs.shape, lambda b, h, i: (0, 0)),
        pl.BlockSpec((tq, HEAD_V), lambda b, h, i: (b * nq + i, h)),
    ]
    args = [lam_vecs, q]
    for k, v, seq_k in kvs:
        in_specs.append(pl.BlockSpec((seq_k, HEAD_V), lambda b, h, i: (b, h)))
        in_specs.append(pl.BlockSpec((seq_k, HEAD_V), lambda b, h, i: (b, h)))
        args += [k, v]
    return pl.pallas_call(
        functools.partial(_diff_attn_kernel, n_src=len(kvs), lam_init=lam_init),
        grid=(batch, width // HEAD_V, nq),
        in_specs=in_specs,
        out_specs=pl.BlockSpec((tq, HEAD_V), lambda b, h, i: (b * nq + i, h)),
        out_shape=jax.ShapeDtypeStruct((t_rows, width), F32),
        compiler_params=_params(("parallel", "parallel", "parallel")),
        name=name,
    )(*args)


def _halo_specs(tm, halo, width, t_rows, col=0):
    per = tm // halo
    last = t_rows // halo - 1
    prev = pl.BlockSpec((halo, width), lambda i, *_: (jnp.maximum(i * per - 1, 0), col))
    nxt = pl.BlockSpec((halo, width), lambda i, *_: (jnp.minimum((i + 1) * per, last), col))
    return prev, nxt


def _seq_pos(tm, seq_len):
    row = lax.broadcasted_iota(jnp.int32, (tm, 1), 0)
    return lax.rem(pl.program_id(0) * tm + row, seq_len)


def _shift_rows(u, u_prev, u_next, pos, seq_len):
    tm = u.shape[0]
    row = lax.broadcasted_iota(jnp.int32, (tm, 1), 0)
    dn = jnp.where(row == 0, u_prev, pltpu.roll(u, 1, axis=0))
    up = jnp.where(row == tm - 1, u_next, pltpu.roll(u, tm - 1, axis=0))
    dn = jnp.where(pos == 0, 0.0, dn)
    up = jnp.where(pos == seq_len - 1, 0.0, up)
    return dn, up


def _head_rms(o, gain, post_scale=1.0):
    parts = []
    for h in range(o.shape[-1] // HEAD_V):
        t = _rms(o[:, h * HEAD_V:(h + 1) * HEAD_V], HEAD_V) * gain
        parts.append(t * post_scale if post_scale != 1.0 else t)
    return parts


def _out_even_kernel(x_ref, gt_ref, gp_ref, gn_ref, o_ref, mod_ref, w_ref, cw_ref, sg_ref, y_ref, *,
                     seq_len, sc_width, post_scale):
    tm = x_ref.shape[0]
    pos = _seq_pos(tm, seq_len)
    w = sc_width
    b_g = gt_ref[:, 0:w]
    u = gt_ref[:, w:2 * w] * gt_ref[:, 2 * w:3 * w]
    u_prev = gp_ref[HALO - 1:HALO, w:2 * w] * gp_ref[HALO - 1:HALO, 2 * w:3 * w]
    u_next = gn_ref[0:1, w:2 * w] * gn_ref[0:1, 2 * w:3 * w]
    dn, up = _shift_rows(u, u_prev, u_next, pos, seq_len)
    y_conv = b_g * (dn * cw_ref[0:1, :] + u * cw_ref[1:2, :] + up * cw_ref[2:3, :])
    parts = [y_conv.astype(BF16)] + [p.astype(BF16) for p in _head_rms(o_ref[...], sg_ref[...], post_scale)]
    z = jnp.concatenate(parts, axis=-1)
    y_ref[...] = x_ref[...] + mod_ref[2:3, :] * _dot(z, w_ref[...])


def _out_even(x, gates, o, mod, w_out, conv_w, subln_g, seq_len, tm, post_scale, name):
    t_rows, d = x.shape
    gw = gates.shape[1]
    per_seq = mod.shape[0] > 1
    tiles_per_seq = max(seq_len // tm, 1)
    mod_map = (lambda i: (i // tiles_per_seq, 0, 0)) if per_seq else (lambda i: (0, 0, 0))
    prev, nxt = _halo_specs(tm, HALO, gw, t_rows)
    return pl.pallas_call(
        functools.partial(_out_even_kernel, seq_len=seq_len, sc_width=gw // 3, post_scale=post_scale),
        grid=(t_rows // tm,),
        in_specs=[
            pl.BlockSpec((tm, d), lambda i: (i, 0)),
            pl.BlockSpec((tm, gw), lambda i: (i, 0)),
            prev, nxt,
            pl.BlockSpec((tm, o.shape[1]), lambda i: (i, 0)),
            pl.BlockSpec((None, 6, d), mod_map),
            pl.BlockSpec(w_out.shape, lambda i: (0, 0)),
            pl.BlockSpec(conv_w.shape, lambda i: (0, 0)),
            pl.BlockSpec((1, HEAD_V), lambda i: (0, 0)),
        ],
        out_specs=pl.BlockSpec((tm, d), lambda i: (i, 0)),
        out_shape=jax.ShapeDtypeStruct((t_rows, d), F32),
        compiler_params=_params(("parallel",)),
        name=name,
    )(x, gates, gates, gates, o, mod, w_out, conv_w, subln_g.reshape(1, HEAD_V))


def _ffn_kernel(x_ref, xp_ref, xn_ref, mod_ref, g_ref, wa_ref, wb_ref, cw_ref, cb_ref, wd_ref, y_ref, h_ref, *,
                seq_len):
    tm = x_ref.shape[0]
    f = pl.program_id(1)

    @pl.when(f == 0)
    def _():
        def nm(v):
            return _norm_mod(v, g_ref[...], mod_ref[3:4, :], mod_ref[4:5, :]).astype(BF16)
        h_ref[0:HALO, :] = nm(xp_ref[...])
        h_ref[HALO:HALO + tm, :] = nm(x_ref[...])
        h_ref[HALO + tm:2 * HALO + tm, :] = nm(xn_ref[...])

    pos = _seq_pos(tm, seq_len)
    rows = tm + 2 * HALO
    a_ext = _dot(h_ref[...], wa_ref[...])
    dn = pltpu.roll(a_ext, 1, axis=0)[HALO:HALO + tm]
    up = pltpu.roll(a_ext, rows - 1, axis=0)[HALO:HALO + tm]
    dn = jnp.where(pos == 0, 0.0, dn)
    up = jnp.where(pos == seq_len - 1, 0.0, up)
    a = dn * cw_ref[0:1, :] + a_ext[HALO:HALO + tm] * cw_ref[1:2, :] + up * cw_ref[2:3, :] + cb_ref[...]
    b = _dot(h_ref[HALO:HALO + tm, :], wb_ref[...])
    gated = (a * _sigmoid(a) * b).astype(BF16)
    part = _dot(gated, wd_ref[...])

    @pl.when(f == 0)
    def _():
        y_ref[...] = part

    @pl.when(f > 0)
    def _():
        y_ref[...] += part

    @pl.when(f == pl.num_programs(1) - 1)
    def _():
        y_ref[...] = x_ref[...] + mod_ref[5:6, :] * y_ref[...]


def _ffn(x, mod, g, w_up, conv_w, conv_b, w_down, seq_len, tm, fc, name):
    t_rows, d = x.shape
    fdim = w_down.shape[0]
    nf = fdim // fc
    per_seq = mod.shape[0] > 1
    tiles_per_seq = max(seq_len // tm, 1)
    mod_map = (lambda i, f: (i // tiles_per_seq, 0, 0)) if per_seq else (lambda i, f: (0, 0, 0))
    prev, nxt = _halo_specs(tm, HALO, d, t_rows)
    return pl.pallas_call(
        functools.partial(_ffn_kernel, seq_len=seq_len),
        grid=(t_rows // tm, nf),
        in_specs=[
            pl.BlockSpec((tm, d), lambda i, f: (i, 0)),
            prev, nxt,
            pl.BlockSpec((None, 6, d), mod_map),
            pl.BlockSpec((1, d), lambda i, f: (0, 0)),
            pl.BlockSpec((d, fc), lambda i, f: (0, f)),
            pl.BlockSpec((d, fc), lambda i, f: (0, nf + f)),
            pl.BlockSpec((3, fc), lambda i, f: (0, f)),
            pl.BlockSpec((1, fc), lambda i, f: (0, f)),
            pl.BlockSpec((fc, d), lambda i, f: (f, 0)),
        ],
        out_specs=pl.BlockSpec((tm, d), lambda i, f: (i, 0)),
        out_shape=jax.ShapeDtypeStruct((t_rows, d), F32),
        scratch_shapes=[pltpu.VMEM((tm + 2 * HALO, d), BF16)],
        compiler_params=_params(("parallel", "arbitrary")),
        name=name,
    )(x, x, x, mod, g.reshape(1, d), w_up, w_up, conv_w, conv_b.reshape(1, fdim), w_down)


def _retention_kernel(df_ref, db_ref, q_ref, k_ref, v_ref, kc_ref, vc_ref, o_ref):
    n = q_ref.shape[0]
    lc = kc_ref.shape[0]
    c = RET_CHUNK
    nc = n // c
    pos = lax.broadcasted_iota(jnp.int32, (c, 1), 0).astype(F32)
    ii = lax.broadcasted_iota(jnp.int32, (c, c), 0)
    jj = lax.broadcasted_iota(jnp.int32, (c, c), 1)
    dij = (ii - jj).astype(F32)
    jpos = lax.broadcasted_iota(jnp.int32, (lc, 1), 0).astype(F32)
    lo = _half_mask((1, LANES))

    for h in range(HEADS):
        slab = slice((h // 2) * LANES, (h // 2 + 1) * LANES)
        vcol = slice(h * HEAD_V, (h + 1) * HEAD_V)
        hm = lo if h % 2 == 0 else jnp.logical_not(lo)
        lg_f = -jnp.exp(jnp.full((1, 1), df_ref[h], F32))
        lg_b = -jnp.exp(jnp.full((1, 1), db_ref[h], F32))
        intra = (jnp.where(dij >= 0, jnp.exp(lg_f * jnp.maximum(dij, 0.0)), 0.0)
                 + jnp.where(dij <= 0, jnp.exp(lg_b * jnp.maximum(-dij, 0.0)), 0.0))
        qdec_f = jnp.exp(lg_f * (pos + 1.0))
        kdec_f = jnp.exp(lg_f * (c - 1.0 - pos))
        cdec_f = jnp.exp(lg_f * float(c))
        qdec_b = jnp.exp(lg_b * (c - pos))
        kdec_b = jnp.exp(lg_b * pos)
        cdec_b = jnp.exp(lg_b * float(c))

        kc = jnp.where(hm, kc_ref[:, slab], 0.0)
        vc = vc_ref[:, vcol].astype(BF16)
        s_f0 = _dot_tn((kc * jnp.exp(lg_f * (lc - 1.0 - jpos))).astype(BF16), vc)
        s_b0 = _dot_tn((kc * jnp.exp(lg_b * jpos)).astype(BF16), vc)

        def load(ci):
            r = pl.multiple_of(ci * c, c)
            q = jnp.where(hm, q_ref[pl.ds(r, c), slab], 0.0)
            k = jnp.where(hm, k_ref[pl.ds(r, c), slab], 0.0)
            v = v_ref[pl.ds(r, c), vcol].astype(BF16)
            return r, q, k, v

        def fwd(ci, s):
            r, q, k, v = load(ci)
            att = (_dot_nt(q.astype(BF16), k.astype(BF16)) * intra).astype(BF16)
            o_ref[pl.ds(r, c), vcol] = _dot(att, v) + _dot((q * qdec_f).astype(BF16), s.astype(BF16))
            return s * cdec_f + _dot_tn((k * kdec_f).astype(BF16), v)

        def bwd(t, s):
            r, q, k, v = load(nc - 1 - t)
            o_ref[pl.ds(r, c), vcol] += _dot((q * qdec_b).astype(BF16), s.astype(BF16))
            return s * cdec_b + _dot_tn((k * kdec_b).astype(BF16), v)

        lax.fori_loop(0, nc, fwd, s_f0)
        lax.fori_loop(0, nc, bwd, s_b0)


def _retention(dec_f, dec_b, q, k, v, kc, vc, seq_len, ctx_len, name):
    t_rows = q.shape[0]
    batch = t_rows // seq_len
    qw, vw = q.shape[1], v.shape[1]
    smem = pl.BlockSpec(memory_space=pltpu.SMEM)
    return pl.pallas_call(
        _retention_kernel,
        grid=(batch,),
        in_specs=[
            smem, smem,
            pl.BlockSpec((seq_len, qw), lambda b: (b, 0)),
            pl.BlockSpec((seq_len, qw), lambda b: (b, 0)),
            pl.BlockSpec((seq_len, vw), lambda b: (b, 0)),
            pl.BlockSpec((ctx_len, qw), lambda b: (b, 0)),
            pl.BlockSpec((ctx_len, vw), lambda b: (b, 0)),
        ],
        out_specs=pl.BlockSpec((seq_len, vw), lambda b: (b, 0)),
        out_shape=jax.ShapeDtypeStruct((t_rows, vw), F32),
        compiler_params=_params(("parallel",)),
        name=name,
    )(dec_f, dec_b, q, k, v, kc, vc)


def _out_odd_kernel(x_ref, pv_ref, pp_ref, pn_ref, gt_ref, o_ref, mod_ref, w_ref, pw_ref, ps_ref, gg_ref, y_ref, *,
                    seq_len):
    tm = x_ref.shape[0]
    r0 = lax.rem(pl.program_id(0) * tm, seq_len)
    pos = r0 + lax.broadcasted_iota(jnp.int32, (tm, 1), 0)
    has_prev = (r0 != 0).astype(F32)
    has_next = (r0 + tm != seq_len).astype(F32)
    rows = tm + 2 * HALO
    parts = []
    for gi, r in enumerate(POOL_RADII):
        col = slice(gi * LANES, (gi + 1) * LANES)
        v = pv_ref[:, col]
        ext = jnp.concatenate([pp_ref[:, col] * has_prev, v, pn_ref[:, col] * has_next], axis=0)
        win = ext
        span = 1
        while span < 2 * r:
            win = win + pltpu.roll(win, rows - span, axis=0)
            span *= 2
        total = pltpu.roll(win, r, axis=0) + pltpu.roll(ext, rows - r, axis=0)
        total = total[HALO:HALO + tm]
        cnt = (jnp.minimum(pos + r + 1, seq_len) - jnp.maximum(pos - r, 0)).astype(F32)
        y = (total / cnt - v).astype(BF16)
        parts.append(_dot(y, pw_ref[gi]))
    y_p = jnp.concatenate(parts, axis=-1) * ps_ref[...]
    gate = gt_ref[...]
    y_r = jnp.concatenate(_head_rms(o_ref[...], gg_ref[...]), axis=-1) * (gate * _sigmoid(gate))
    z = jnp.concatenate([y_p.astype(BF16), y_r.astype(BF16)], axis=-1)
    y_ref[...] = x_ref[...] + mod_ref[2:3, :] * _dot(z, w_ref[...])


def _out_odd(x, pv, gate, o, mod, w_out, pool_w, pool_scale, gn_g, seq_len, tm, name):
    t_rows, d = x.shape
    pw = pv.shape[1]
    assert seq_len % tm == 0
    tiles_per_seq = seq_len // tm
    prev, nxt = _halo_specs(tm, HALO, pw, t_rows)
    return pl.pallas_call(
        functools.partial(_out_odd_kernel, seq_len=seq_len),
        grid=(t_rows // tm,),
        in_specs=[
            pl.BlockSpec((tm, d), lambda i: (i, 0)),
            pl.BlockSpec((tm, pw), lambda i: (i, 0)),
            prev, nxt,
            pl.BlockSpec((tm, gate.shape[1]), lambda i: (i, 0)),
            pl.BlockSpec((tm, o.shape[1]), lambda i: (i, 0)),
            pl.BlockSpec((None, 6, d), lambda i: (i // tiles_per_seq, 0, 0)),
            pl.BlockSpec(w_out.shape, lambda i: (0, 0)),
            pl.BlockSpec(pool_w.shape, lambda i: (0, 0, 0)),
            pl.BlockSpec((1, pw), lambda i: (0, 0)),
            pl.BlockSpec((1, HEAD_V), lambda i: (0, 0)),
        ],
        out_specs=pl.BlockSpec((tm, d), lambda i: (i, 0)),
        out_shape=jax.ShapeDtypeStruct((t_rows, d), F32),
        compiler_params=_params(("parallel",)),
        name=name,
    )(x, pv, pv, pv, gate, o, mod, w_out, pool_w, pool_scale.reshape(1, pw), gn_g.reshape(1, HEAD_V))


def _rope_tables(seq_len):
    rows = seq_len // GRID_W
    row = jnp.repeat(jnp.arange(rows, dtype=F32), GRID_W)
    col = jnp.tile(jnp.arange(GRID_W, dtype=F32), rows)
    inv = ROPE_BASE ** (-jnp.arange(ROPE_FREQS, dtype=F32) / ROPE_FREQS)
    ar, ac = row[:, None] * inv, col[:, None] * inv
    cos = jnp.concatenate([jnp.cos(ar), jnp.cos(ar), jnp.cos(ac), jnp.cos(ac)], axis=-1)
    sin = jnp.concatenate([-jnp.sin(ar), jnp.sin(ar), -jnp.sin(ac), jnp.sin(ac)], axis=-1)
    reps = LANES // ROPE_HEAD_DIM
    return jnp.tile(cos, (1, reps)), jnp.tile(sin, (1, reps))


def _tile(total, want):
    t = min(want, total)
    assert total % t == 0
    return t


def kernel(x, c, ctx, c_ctx, mod_w, mod_b, norm1_g, norm2_g, ev_w_in, ev_w_out, sc_conv_w, da_q_norm, da_k_norm,
           da_lq1, da_lk1, da_lq2, da_lk2, da_subln_g, od_w_in, od_w_out, pool_w, pool_scale, ret_decay_f,
           ret_decay_b, ret_gn_g, ffn_w_up, ffn_conv_w, ffn_conv_b, ffn_w_down):
    batch, seq, d = x.shape
    ctx_len = ctx.shape[1]
    depth = mod_w.shape[0]
    t_lat, t_ctx = batch * seq, batch * ctx_len
    xs = x.reshape(t_lat, d)
    cs = ctx.reshape(t_ctx, d)

    pad = (-(batch + 1)) % 8
    c_all = jnp.concatenate([c, c_ctx[None, :], jnp.zeros((pad, d), F32)], axis=0)
    mods = _modulation(c_all, mod_w, mod_b)
    rope = _rope_tables(seq)

    tm_lat = _tile(seq, 512)
    tm_ctx = _tile(t_ctx, 512)
    tm_ffn_lat = _tile(seq, 1024)
    tm_ffn_ctx = _tile(t_ctx, 1024)
    tq = _tile(seq, 256)
    fdim = ffn_w_down.shape[1]
    fc = 256 if fdim % 256 == 0 else LANES

    for i in range(depth):
        last = i == depth - 1
        j = i // 2
        mod_l = mods[i, :batch].reshape(batch, 6, d)
        mod_c = mods[i, batch:batch + 1].reshape(1, 6, d)
        w_up = ffn_w_up[i].astype(BF16)
        w_down = ffn_w_down[i].astype(BF16)
        if i % 2 == 0:
            w_in = ev_w_in[j].astype(BF16)
            w_out = ev_w_out[j].astype(BF16)
            scw = ev_w_out.shape[1] // 2
            qkw = (w_in.shape[1] - 4 * scw) // 2
            segs = [(0, 3 * scw, "copy", 0, 1.0),
                    (3 * scw, 3 * scw + qkw, "qknorm", 0, ROPE_HEAD_DIM ** -0.5),
                    (3 * scw + qkw, 3 * scw + 2 * qkw, "qknorm", 1, 1.0),
                    (3 * scw + 2 * qkw, w_in.shape[1], "copy", 0, 1.0)]
            dts = [F32, BF16, BF16, BF16]
            gains = [da_q_norm[j], da_k_norm[j]]
            lam_init = 0.8 - 0.6 * math.exp(-0.3 * i)
            lam_vecs = jnp.stack([da_lq1[j], da_lk1[j], da_lq2[j], da_lk2[j]], axis=0)
            g_l, q_l, k_l, v_l = _in_proj(xs, mod_l, norm1_g[i], w_in, segs, dts, gains, rope, seq, tm_lat,
                                          "in_even_lat")
            g_c, q_c, k_c, v_c = _in_proj(cs, mod_c, norm1_g[i], w_in, segs, dts, gains, None, ctx_len, tm_ctx,
                                          "in_even_ctx")
            o_l = _diff_attn(lam_vecs, q_l, [(k_l, v_l, seq), (k_c, v_c, ctx_len)], seq, tq, lam_init, "attn_lat")
            xs_new = _out_even(xs, g_l, o_l, mod_l, w_out, sc_conv_w[j], da_subln_g[j], seq, tm_lat,
                               1.0 - lam_init, "out_even_lat")
            if not last:
                o_c = _diff_attn(lam_vecs, q_c, [(k_c, v_c, ctx_len)], ctx_len, ctx_len, lam_init, "attn_ctx")
                cs = _out_even(cs, g_c, o_c, mod_c, w_out, sc_conv_w[j], da_subln_g[j], ctx_len, tm_ctx,
                               1.0 - lam_init, "out_even_ctx")
            xs = xs_new
        else:
            assert last, "an odd layer that still updates the context stream is not implemented"
            w_in = od_w_in[j].astype(BF16)
            w_out = od_w_out[j].astype(BF16)
            pw = pool_scale.shape[1]
            vw = HEADS * HEAD_V
            qw = (w_in.shape[1] - pw - 2 * vw) // 2
            segs = [(0, pw, "copy", 0, 1.0),
                    (pw, pw + qw, "rope", 0, ROPE_HEAD_DIM ** -0.5),
                    (pw + qw, pw + 2 * qw, "rope", 0, 1.0),
                    (pw + 2 * qw, pw + 2 * qw + vw, "copy", 0, 1.0),
                    (pw + 2 * qw + vw, w_in.shape[1], "copy", 0, 1.0)]
            pv, q_l, k_l, v_l, gate = _in_proj(xs, mod_l, norm1_g[i], w_in, segs, [F32] * 5, [], rope, seq, tm_lat,
                                               "in_odd_lat")
            w_kv = w_in[:, pw + qw:pw + 2 * qw + vw]
            k_c, v_c = _in_proj(cs, mod_c, norm1_g[i], w_kv, [(0, qw, "copy", 0, 1.0), (qw, qw + vw, "copy", 0, 1.0)],
                                [F32, F32], [], None, ctx_len, tm_ctx, "in_odd_ctx")
            o_l = _retention(ret_decay_f[j], ret_decay_b[j], q_l, k_l, v_l, k_c, v_c, seq, ctx_len, "retention")
            xs = _out_odd(xs, pv, gate, o_l, mod_l, w_out, pool_w[j].astype(BF16), pool_scale[j], ret_gn_g[j],
                          seq, tm_lat, "out_odd_lat")
        xs = _ffn(xs, mod_l, norm2_g[i], w_up, ffn_conv_w[i], ffn_conv_b[i], w_down, seq, tm_ffn_lat, fc, "ffn_lat")
        if not last:
            cs = _ffn(cs, mod_c, norm2_g[i], w_up, ffn_conv_w[i], ffn_conv_b[i], w_down, ctx_len, tm_ffn_ctx, fc,
                      "ffn_ctx")
    return xs.reshape(batch, seq, d)
```

```python
import functools
import math

import jax
import jax.numpy as jnp
from jax import lax
from jax.experimental import pallas as pl
from jax.experimental.pallas import tpu as pltpu

F32 = jnp.float32
BF16 = jnp.bfloat16

EPS = 1e-6
GRID_W = 64
ROPE_HEAD_DIM = 64
ROPE_FREQS = ROPE_HEAD_DIM // 4
ROPE_BASE = 10000.0
HEADS = 4
HEAD_V = 128
RET_CHUNK = 128
POOL_RADII = (1, 2, 4, 8)
LANES = 128
HALO = 16
VMEM_LIMIT = 56 * 1024 * 1024
LOG2E = math.log2(math.e)
BOUND_MARGIN = 1.02
MAX_SAFE_SCORE_BOUND = 40.0


def _params(sem, vmem=VMEM_LIMIT):
    return pltpu.CompilerParams(dimension_semantics=sem, vmem_limit_bytes=vmem)


def _sigmoid(x):
    return 1.0 / (1.0 + jnp.exp(-x))


def _dot(a, b):
    return jnp.dot(a, b, preferred_element_type=F32)


def _dot_nt(a, b):
    return lax.dot_general(a, b, (((1,), (1,)), ((), ())), preferred_element_type=F32)


def _dot_tn(a, b):
    return lax.dot_general(a, b, (((0,), (0,)), ((), ())), preferred_element_type=F32)


def _rms(x, width):
    return x * lax.rsqrt(jnp.sum(x * x, axis=-1, keepdims=True) * (1.0 / width) + EPS)


def _norm_mod(x, g, shift, scale):
    return (_rms(x, x.shape[-1]) * g) * (1.0 + scale) + shift


def _half_mask(shape):
    return lax.broadcasted_iota(jnp.int32, shape, len(shape) - 1) < (LANES // 2)


def _rope(t, cos, sin):
    lane = lax.broadcasted_iota(jnp.int32, t.shape, 1)
    first = lax.rem(lane, 2 * ROPE_FREQS) < ROPE_FREQS
    partner = jnp.where(first, pltpu.roll(t, LANES - ROPE_FREQS, axis=1), pltpu.roll(t, ROPE_FREQS, axis=1))
    return t * cos + partner * sin


def _norm64(t, gain):
    lo = _half_mask(t.shape)
    sq = t * t
    s_lo = jnp.sum(jnp.where(lo, sq, 0.0), axis=-1, keepdims=True)
    s_hi = jnp.sum(jnp.where(lo, 0.0, sq), axis=-1, keepdims=True)
    w = 1.0 / (LANES // 2)
    r = jnp.where(lo, lax.rsqrt(s_lo * w + EPS), lax.rsqrt(s_hi * w + EPS))
    return (t * r) * gain


def _mod_kernel(c_ref, w_ref, b_ref, o_ref):
    c = c_ref[...]
    a = c * _sigmoid(c)
    o_ref[...] = jnp.dot(a, w_ref[...], preferred_element_type=F32,
                         precision=lax.Precision.HIGHEST) + b_ref[...]


def _modulation(c_all, mod_w, mod_b):
    depth, d, n6 = mod_w.shape
    rows = c_all.shape[0]
    tn = 1536
    return pl.pallas_call(
        _mod_kernel,
        grid=(depth, n6 // tn),
        in_specs=[
            pl.BlockSpec((rows, d), lambda l, j: (0, 0)),
            pl.BlockSpec((None, d, tn), lambda l, j: (l, 0, j)),
            pl.BlockSpec((None, 1, tn), lambda l, j: (l, 0, j)),
        ],
        out_specs=pl.BlockSpec((None, rows, tn), lambda l, j: (l, 0, j)),
        out_shape=jax.ShapeDtypeStruct((depth, rows, n6), F32),
        compiler_params=_params(("parallel", "parallel")),
        name="modulation",
    )(c_all, mod_w, mod_b.reshape(depth, 1, n6))


def _in_proj_kernel(*refs, segs, rotate, n_gain):
    x_ref, mod_ref, g_ref, w_ref = refs[:4]
    gain_refs = refs[4:4 + n_gain]
    pos = 4 + n_gain
    if rotate:
        cos_ref, sin_ref = refs[pos:pos + 2]
        pos += 2
    out_refs = refs[pos:]
    h = _norm_mod(x_ref[...], g_ref[...], mod_ref[0:1, :], mod_ref[1:2, :]).astype(BF16)
    for (c0, c1, mode, gain_idx, scale), o_ref in zip(segs, out_refs):
        y = _dot(h, w_ref[:, c0:c1])
        if mode == "copy":
            o_ref[...] = y.astype(o_ref.dtype)
            continue
        for s in range((c1 - c0) // LANES):
            t = y[:, s * LANES:(s + 1) * LANES]
            if mode == "qknorm":
                t = _norm64(t, gain_refs[gain_idx][...])
            if rotate:
                t = _rope(t, cos_ref[...], sin_ref[...])
            if scale != 1.0:
                t = t * scale
            o_ref[:, s * LANES:(s + 1) * LANES] = t.astype(o_ref.dtype)


def _in_proj(x, mod, g, w, segs, out_dtypes, gains, rope, seq_len, tm, name):
    t_rows, d = x.shape
    per_seq = mod.shape[0] > 1
    tiles_per_seq = max(seq_len // tm, 1)
    if per_seq:
        assert seq_len % tm == 0
        mod_map = lambda i: (i // tiles_per_seq, 0, 0)
    else:
        mod_map = lambda i: (0, 0, 0)
    in_specs = [
        pl.BlockSpec((tm, d), lambda i: (i, 0)),
        pl.BlockSpec((None, 6, d), mod_map),
        pl.BlockSpec((1, d), lambda i: (0, 0)),
        pl.BlockSpec(w.shape, lambda i: (0, 0)),
    ]
    args = [x, mod, g.reshape(1, d), w]
    for gn in gains:
        in_specs.append(pl.BlockSpec((1, LANES), lambda i: (0, 0)))
        args.append(jnp.tile(gn.reshape(1, -1), (1, LANES // gn.shape[-1])))
    if rope is not None:
        assert seq_len % tm == 0
        for tab in rope:
            in_specs.append(pl.BlockSpec((tm, LANES), lambda i: (i % tiles_per_seq, 0)))
            args.append(tab)
    out_specs = [pl.BlockSpec((tm, c1 - c0), lambda i: (i, 0)) for (c0, c1, _, _, _) in segs]
    out_shape = [jax.ShapeDtypeStruct((t_rows, c1 - c0), dt) for (c0, c1, _, _, _), dt in zip(segs, out_dtypes)]
    return pl.pallas_call(
        functools.partial(_in_proj_kernel, segs=tuple(segs), rotate=rope is not None, n_gain=len(gains)),
        grid=(t_rows // tm,),
        in_specs=in_specs,
        out_specs=out_specs,
        out_shape=out_shape,
        compiler_params=_params(("parallel",)),
        name=name,
    )(*args)


def _diff_attn_kernel(*refs, n_src, lam_init):
    bound_ref, lam_ref, q_ref = refs[:3]
    kv_refs = refs[3:3 + 2 * n_src]
    o_ref = refs[3 + 2 * n_src]
    lv = lam_ref[...]
    lam = (jnp.exp(jnp.sum(lv[0:1] * lv[1:2], axis=-1, keepdims=True))
           - jnp.exp(jnp.sum(lv[2:3] * lv[3:4], axis=-1, keepdims=True)) + lam_init)
    bound = bound_ref[0]

    def attend(use_bound):
        q = q_ref[...]
        s0, s1, vs = [], [], []
        for j in range(n_src):
            k = kv_refs[2 * j][...]
            lo = _half_mask(k.shape)
            zero = jnp.zeros_like(k)
            s0.append(_dot_nt(q, jnp.where(lo, k, zero)))
            s1.append(_dot_nt(q, jnp.where(lo, zero, k)))
            vs.append(kv_refs[2 * j + 1][...])

        def softmax_parts(ss):
            if use_bound:
                m = bound
            else:
                m = functools.reduce(jnp.maximum, [jnp.max(s, axis=-1, keepdims=True) for s in ss])
            ps = [jnp.exp2(s - m) for s in ss]
            l = functools.reduce(jnp.add, [jnp.sum(p, axis=-1, keepdims=True) for p in ps])
            return ps, l

        p0, l0 = softmax_parts(s0)
        p1, l1 = softmax_parts(s1)
        ratio = lam * l0 / l1
        o = None
        for j in range(n_src):
            p = (p0[j] - p1[j] * ratio).astype(BF16)
            t = _dot(p, vs[j])
            o = t if o is None else o + t
        o_ref[...] = (o * (1.0 / l0)).astype(o_ref.dtype)

    safe = bound <= MAX_SAFE_SCORE_BOUND

    @pl.when(safe)
    def _():
        attend(True)

    @pl.when(jnp.logical_not(safe))
    def _():
        attend(False)


def _diff_attn(bound, lam_vecs, q, kvs, seq_q, tq, lam_init, name):
    t_rows, width = q.shape
    batch = t_rows // seq_q
    nq = seq_q // tq
    in_specs = [
        pl.BlockSpec(memory_space=pltpu.SMEM),
        pl.BlockSpec(lam_vecs.shape, lambda b, h, i: (0, 0)),
        pl.BlockSpec((tq, HEAD_V), lambda b, h, i: (b * nq + i, h)),
    ]
    args = [bound, lam_vecs, q]
    for k, v, seq_k in kvs:
        in_specs.append(pl.BlockSpec((seq_k, HEAD_V), lambda b, h, i: (b, h)))
        in_specs.append(pl.BlockSpec((seq_k, HEAD_V), lambda b, h, i: (b, h)))
        args += [k, v]
    return pl.pallas_call(
        functools.partial(_diff_attn_kernel, n_src=len(kvs), lam_init=lam_init),
        grid=(batch, width // HEAD_V, nq),
        in_specs=in_specs,
        out_specs=pl.BlockSpec((tq, HEAD_V), lambda b, h, i: (b * nq + i, h)),
        out_shape=jax.ShapeDtypeStruct((t_rows, width), F32),
        compiler_params=_params(("parallel", "parallel", "parallel")),
        name=name,
    )(*args)


def _halo_specs(tm, halo, width, t_rows, col=0):
    per = tm // halo
    last = t_rows // halo - 1
    prev = pl.BlockSpec((halo, width), lambda i, *_: (jnp.maximum(i * per - 1, 0), col))
    nxt = pl.BlockSpec((halo, width), lambda i, *_: (jnp.minimum((i + 1) * per, last), col))
    return prev, nxt


def _seq_pos(tm, seq_len):
    row = lax.broadcasted_iota(jnp.int32, (tm, 1), 0)
    return lax.rem(pl.program_id(0) * tm + row, seq_len)


def _shift_rows(u, u_prev, u_next, pos, seq_len):
    tm = u.shape[0]
    row = lax.broadcasted_iota(jnp.int32, (tm, 1), 0)
    dn = jnp.where(row == 0, u_prev, pltpu.roll(u, 1, axis=0))
    up = jnp.where(row == tm - 1, u_next, pltpu.roll(u, tm - 1, axis=0))
    dn = jnp.where(pos == 0, 0.0, dn)
    up = jnp.where(pos == seq_len - 1, 0.0, up)
    return dn, up


def _head_rms(o, gain, post_scale=1.0):
    parts = []
    for h in range(o.shape[-1] // HEAD_V):
        t = _rms(o[:, h * HEAD_V:(h + 1) * HEAD_V], HEAD_V) * gain
        parts.append(t * post_scale if post_scale != 1.0 else t)
    return parts


def _out_even_kernel(x_ref, gt_ref, gp_ref, gn_ref, o_ref, mod_ref, w_ref, cw_ref, sg_ref, y_ref, *,
                     seq_len, sc_width, post_scale):
    tm = x_ref.shape[0]
    pos = _seq_pos(tm, seq_len)
    w = sc_width
    b_g = gt_ref[:, 0:w]
    u = gt_ref[:, w:2 * w] * gt_ref[:, 2 * w:3 * w]
    u_prev = gp_ref[HALO - 1:HALO, w:2 * w] * gp_ref[HALO - 1:HALO, 2 * w:3 * w]
    u_next = gn_ref[0:1, w:2 * w] * gn_ref[0:1, 2 * w:3 * w]
    dn, up = _shift_rows(u, u_prev, u_next, pos, seq_len)
    y_conv = b_g * (dn * cw_ref[0:1, :] + u * cw_ref[1:2, :] + up * cw_ref[2:3, :])
    parts = [y_conv.astype(BF16)] + [p.astype(BF16) for p in _head_rms(o_ref[...], sg_ref[...], post_scale)]
    z = jnp.concatenate(parts, axis=-1)
    y_ref[...] = x_ref[...] + mod_ref[2:3, :] * _dot(z, w_ref[...])


def _out_even(x, gates, o, mod, w_out, conv_w, subln_g, seq_len, tm, post_scale, name):
    t_rows, d = x.shape
    gw = gates.shape[1]
    per_seq = mod.shape[0] > 1
    tiles_per_seq = max(seq_len // tm, 1)
    mod_map = (lambda i: (i // tiles_per_seq, 0, 0)) if per_seq else (lambda i: (0, 0, 0))
    prev, nxt = _halo_specs(tm, HALO, gw, t_rows)
    return pl.pallas_call(
        functools.partial(_out_even_kernel, seq_len=seq_len, sc_width=gw // 3, post_scale=post_scale),
        grid=(t_rows // tm,),
        in_specs=[
            pl.BlockSpec((tm, d), lambda i: (i, 0)),
            pl.BlockSpec((tm, gw), lambda i: (i, 0)),
            prev, nxt,
            pl.BlockSpec((tm, o.shape[1]), lambda i: (i, 0)),
            pl.BlockSpec((None, 6, d), mod_map),
            pl.BlockSpec(w_out.shape, lambda i: (0, 0)),
            pl.BlockSpec(conv_w.shape, lambda i: (0, 0)),
            pl.BlockSpec((1, HEAD_V), lambda i: (0, 0)),
        ],
        out_specs=pl.BlockSpec((tm, d), lambda i: (i, 0)),
        out_shape=jax.ShapeDtypeStruct((t_rows, d), F32),
        compiler_params=_params(("parallel",)),
        name=name,
    )(x, gates, gates, gates, o, mod, w_out, conv_w, subln_g.reshape(1, HEAD_V))


def _ffn_kernel(x_ref, xp_ref, xn_ref, mod_ref, g_ref, wa_ref, wb_ref, cw_ref, cb_ref, wd_ref, y_ref, h_ref, *,
                seq_len):
    tm = x_ref.shape[0]
    f = pl.program_id(1)

    multi_seq = tm > seq_len

    @pl.when(f == 0)
    def _():
        def nm(v):
            return _norm_mod(v, g_ref[...], mod_ref[3:4, :], mod_ref[4:5, :])
        if multi_seq:
            keep_prev = keep_next = 1.0
        else:
            r0 = lax.rem(pl.program_id(0) * tm, seq_len)
            keep_prev = (r0 != 0).astype(F32)
            keep_next = (r0 + tm != seq_len).astype(F32)
        h_ref[0:HALO, :] = (nm(xp_ref[...]) * keep_prev).astype(BF16)
        h_ref[HALO:HALO + tm, :] = nm(x_ref[...]).astype(BF16)
        h_ref[HALO + tm:2 * HALO + tm, :] = (nm(xn_ref[...]) * keep_next).astype(BF16)
        y_ref[...] = jnp.zeros_like(y_ref)

    rows = tm + 2 * HALO
    a_ext = _dot(h_ref[...], wa_ref[...])
    dn = pltpu.roll(a_ext, 1, axis=0)[HALO:HALO + tm]
    up = pltpu.roll(a_ext, rows - 1, axis=0)[HALO:HALO + tm]
    if multi_seq:
        pos = _seq_pos(tm, seq_len)
        dn = jnp.where(pos == 0, 0.0, dn)
        up = jnp.where(pos == seq_len - 1, 0.0, up)
    a = dn * cw_ref[0:1, :] + a_ext[HALO:HALO + tm] * cw_ref[1:2, :] + up * cw_ref[2:3, :] + cb_ref[...]
    b = _dot(h_ref[HALO:HALO + tm, :], wb_ref[...])
    gated = (a * _sigmoid(a) * b).astype(BF16)
    y_ref[...] += _dot(gated, wd_ref[...])

    @pl.when(f == pl.num_programs(1) - 1)
    def _():
        y_ref[...] = x_ref[...] + mod_ref[5:6, :] * y_ref[...]


def _ffn(x, mod, g, w_up, conv_w, conv_b, w_down, seq_len, tm, fc, name):
    t_rows, d = x.shape
    fdim = w_down.shape[0]
    nf = fdim // fc
    per_seq = mod.shape[0] > 1
    tiles_per_seq = max(seq_len // tm, 1)
    mod_map = (lambda i, f: (i // tiles_per_seq, 0, 0)) if per_seq else (lambda i, f: (0, 0, 0))
    prev, nxt = _halo_specs(tm, HALO, d, t_rows)
    return pl.pallas_call(
        functools.partial(_ffn_kernel, seq_len=seq_len),
        grid=(t_rows // tm, nf),
        in_specs=[
            pl.BlockSpec((tm, d), lambda i, f: (i, 0)),
            prev, nxt,
            pl.BlockSpec((None, 6, d), mod_map),
            pl.BlockSpec((1, d), lambda i, f: (0, 0)),
            pl.BlockSpec((d, fc), lambda i, f: (0, f)),
            pl.BlockSpec((d, fc), lambda i, f: (0, nf + f)),
            pl.BlockSpec((3, fc), lambda i, f: (0, f)),
            pl.BlockSpec((1, fc), lambda i, f: (0, f)),
            pl.BlockSpec((fc, d), lambda i, f: (f, 0)),
        ],
        out_specs=pl.BlockSpec((tm, d), lambda i, f: (i, 0)),
        out_shape=jax.ShapeDtypeStruct((t_rows, d), F32),
        scratch_shapes=[pltpu.VMEM((tm + 2 * HALO, d), BF16)],
        compiler_params=_params(("parallel", "arbitrary")),
        name=name,
    )(x, x, x, mod, g.reshape(1, d), w_up, w_up, conv_w, conv_b.reshape(1, fdim), w_down)


def _retention_kernel(df_ref, db_ref, q_ref, k_ref, v_ref, kc_ref, vc_ref, o_ref, kv_ref, ss_ref):
    n = q_ref.shape[0]
    lc = kc_ref.shape[0]
    c = RET_CHUNK
    nc = n // c
    pos = lax.broadcasted_iota(jnp.int32, (c, 1), 0).astype(F32)
    ii = lax.broadcasted_iota(jnp.int32, (c, c), 0)
    jj = lax.broadcasted_iota(jnp.int32, (c, c), 1)
    dij = (ii - jj).astype(F32)
    jpos = lax.broadcasted_iota(jnp.int32, (lc, 1), 0).astype(F32)
    lo = _half_mask((1, LANES))
    row_lo = lax.broadcasted_iota(jnp.int32, (LANES, 1), 0) < (LANES // 2)
    block_diag = (lax.broadcasted_iota(jnp.int32, (LANES, 2 * HEAD_V), 0) // (LANES // 2)
                  == lax.broadcasted_iota(jnp.int32, (LANES, 2 * HEAD_V), 1) // HEAD_V)
    zeros_v = jnp.zeros((c, HEAD_V), BF16)

    def cat(a, b):
        return jnp.concatenate([a, b], axis=-1)

    for pair in range(HEADS // 2):
        slab = slice(pair * LANES, (pair + 1) * LANES)
        vcol = slice(pair * 2 * HEAD_V, (pair + 1) * 2 * HEAD_V)
        lg = []
        for d_ref in (df_ref, db_ref):
            lg.append([-jnp.exp(jnp.full((1, 1), d_ref[2 * pair + e], F32)) for e in range(2)])
        (lgf0, lgf1), (lgb0, lgb1) = lg
        lgf_lane = jnp.where(lo, lgf0, lgf1)
        lgb_lane = jnp.where(lo, lgb0, lgb1)
        lgf_row = jnp.where(row_lo, lgf0, lgf1)
        lgb_row = jnp.where(row_lo, lgb0, lgb1)
        qdec = cat(jnp.exp(lgf_lane * (pos + 1.0)), jnp.exp(lgb_lane * (c - pos)))
        kdec = cat(jnp.exp(lgf_lane * (c - 1.0 - pos)), jnp.exp(lgb_lane * pos))
        cdec_f = jnp.exp(lgf_row * float(c))
        cdec_b = jnp.exp(lgb_row * float(c))
        intra = [jnp.where(dij >= 0, jnp.exp(f * jnp.maximum(dij, 0.0)), 0.0)
                 + jnp.where(dij <= 0, jnp.exp(b * jnp.maximum(-dij, 0.0)), 0.0)
                 for f, b in ((lgf0, lgb0), (lgf1, lgb1))]

        for ci in range(nc):
            rows = slice(ci * c, (ci + 1) * c)
            k = k_ref[rows, slab]
            kv_ref[ci] = _dot_tn((cat(k, k) * kdec).astype(BF16), v_ref[rows, vcol].astype(BF16))
        kc = kc_ref[:, slab]
        cdec0 = cat(jnp.exp(lgf_lane * (lc - 1.0 - jpos)), jnp.exp(lgb_lane * jpos))
        s0 = _dot_tn((cat(kc, kc) * cdec0).astype(BF16), vc_ref[:, vcol].astype(BF16))

        s = s0[0:LANES]
        for ci in range(nc):
            ss_ref[ci, 0:LANES, :] = jnp.where(block_diag, s, 0.0).astype(BF16)
            s = s * cdec_f + kv_ref[ci, 0:LANES, :]
        s = s0[LANES:2 * LANES]
        for ci in reversed(range(nc)):
            ss_ref[ci, LANES:2 * LANES, :] = jnp.where(block_diag, s, 0.0).astype(BF16)
            s = s * cdec_b + kv_ref[ci, LANES:2 * LANES, :]

        for ci in range(nc):
            rows = slice(ci * c, (ci + 1) * c)
            q = q_ref[rows, slab]
            kb = k_ref[rows, slab].astype(BF16)
            v = v_ref[rows, vcol].astype(BF16)
            att0 = _dot_nt(jnp.where(lo, q, 0.0).astype(BF16), kb) * intra[0]
            att1 = _dot_nt(jnp.where(lo, 0.0, q).astype(BF16), kb) * intra[1]
            v_blocks = jnp.concatenate([cat(v[:, 0:HEAD_V], zeros_v), cat(zeros_v, v[:, HEAD_V:])], axis=0)
            o_ref[rows, vcol] = (_dot(cat(att0, att1).astype(BF16), v_blocks)
                                 + _dot((cat(q, q) * qdec).astype(BF16), ss_ref[ci]))


def _retention(dec_f, dec_b, q, k, v, kc, vc, seq_len, ctx_len, name):
    t_rows = q.shape[0]
    batch = t_rows // seq_len
    qw, vw = q.shape[1], v.shape[1]
    nc = seq_len // RET_CHUNK
    smem = pl.BlockSpec(memory_space=pltpu.SMEM)
    return pl.pallas_call(
        _retention_kernel,
        grid=(batch,),
        in_specs=[
            smem, smem,
            pl.BlockSpec((seq_len, qw), lambda b: (b, 0)),
            pl.BlockSpec((seq_len, qw), lambda b: (b, 0)),
            pl.BlockSpec((seq_len, vw), lambda b: (b, 0)),
            pl.BlockSpec((ctx_len, qw), lambda b: (b, 0)),
            pl.BlockSpec((ctx_len, vw), lambda b: (b, 0)),
        ],
        out_specs=pl.BlockSpec((seq_len, vw), lambda b: (b, 0)),
        out_shape=jax.ShapeDtypeStruct((t_rows, vw), F32),
        scratch_shapes=[pltpu.VMEM((nc, 2 * LANES, 2 * HEAD_V), F32),
                        pltpu.VMEM((nc, 2 * LANES, 2 * HEAD_V), BF16)],
        compiler_params=_params(("parallel",)),
        name=name,
    )(dec_f, dec_b, q, k, v, kc, vc)


def _out_odd_kernel(x_ref, pv_ref, pp_ref, pn_ref, gt_ref, o_ref, mod_ref, w_ref, pw_ref, ps_ref, gg_ref, y_ref, *,
                    seq_len):
    tm = x_ref.shape[0]
    r0 = lax.rem(pl.program_id(0) * tm, seq_len)
    pos = r0 + lax.broadcasted_iota(jnp.int32, (tm, 1), 0)
    has_prev = (r0 != 0).astype(F32)
    has_next = (r0 + tm != seq_len).astype(F32)
    rows = tm + 2 * HALO
    parts = []
    for gi, r in enumerate(POOL_RADII):
        col = slice(gi * LANES, (gi + 1) * LANES)
        v = pv_ref[:, col]
        ext = jnp.concatenate([pp_ref[:, col] * has_prev, v, pn_ref[:, col] * has_next], axis=0)
        win = ext
        span = 1
        while span < 2 * r:
            win = win + pltpu.roll(win, rows - span, axis=0)
            span *= 2
        total = pltpu.roll(win, r, axis=0) + pltpu.roll(ext, rows - r, axis=0)
        total = total[HALO:HALO + tm]
        cnt = (jnp.minimum(pos + r + 1, seq_len) - jnp.maximum(pos - r, 0)).astype(F32)
        y = (total / cnt - v).astype(BF16)
        parts.append(_dot(y, pw_ref[gi]))
    y_p = jnp.concatenate(parts, axis=-1) * ps_ref[...]
    gate = gt_ref[...]
    y_r = jnp.concatenate(_head_rms(o_ref[...], gg_ref[...]), axis=-1) * (gate * _sigmoid(gate))
    z = jnp.concatenate([y_p.astype(BF16), y_r.astype(BF16)], axis=-1)
    y_ref[...] = x_ref[...] + mod_ref[2:3, :] * _dot(z, w_ref[...])


def _out_odd(x, pv, gate, o, mod, w_out, pool_w, pool_scale, gn_g, seq_len, tm, name):
    t_rows, d = x.shape
    pw = pv.shape[1]
    assert seq_len % tm == 0
    tiles_per_seq = seq_len // tm
    prev, nxt = _halo_specs(tm, HALO, pw, t_rows)
    return pl.pallas_call(
        functools.partial(_out_odd_kernel, seq_len=seq_len),
        grid=(t_rows // tm,),
        in_specs=[
            pl.BlockSpec((tm, d), lambda i: (i, 0)),
            pl.BlockSpec((tm, pw), lambda i: (i, 0)),
            prev, nxt,
            pl.BlockSpec((tm, gate.shape[1]), lambda i: (i, 0)),
            pl.BlockSpec((tm, o.shape[1]), lambda i: (i, 0)),
            pl.BlockSpec((None, 6, d), lambda i: (i // tiles_per_seq, 0, 0)),
            pl.BlockSpec(w_out.shape, lambda i: (0, 0)),
            pl.BlockSpec(pool_w.shape, lambda i: (0, 0, 0)),
            pl.BlockSpec((1, pw), lambda i: (0, 0)),
            pl.BlockSpec((1, HEAD_V), lambda i: (0, 0)),
        ],
        out_specs=pl.BlockSpec((tm, d), lambda i: (i, 0)),
        out_shape=jax.ShapeDtypeStruct((t_rows, d), F32),
        compiler_params=_params(("parallel",)),
        name=name,
    )(x, pv, pv, pv, gate, o, mod, w_out, pool_w, pool_scale.reshape(1, pw), gn_g.reshape(1, HEAD_V))


def _rope_tables(seq_len):
    rows = seq_len // GRID_W
    row = jnp.repeat(jnp.arange(rows, dtype=F32), GRID_W)
    col = jnp.tile(jnp.arange(GRID_W, dtype=F32), rows)
    inv = ROPE_BASE ** (-jnp.arange(ROPE_FREQS, dtype=F32) / ROPE_FREQS)
    ar, ac = row[:, None] * inv, col[:, None] * inv
    cos = jnp.concatenate([jnp.cos(ar), jnp.cos(ar), jnp.cos(ac), jnp.cos(ac)], axis=-1)
    sin = jnp.concatenate([-jnp.sin(ar), jnp.sin(ar), -jnp.sin(ac), jnp.sin(ac)], axis=-1)
    reps = LANES // ROPE_HEAD_DIM
    return jnp.tile(cos, (1, reps)), jnp.tile(sin, (1, reps))


def _tile(total, want):
    t = min(want, total)
    assert total % t == 0
    return t


def kernel(x, c, ctx, c_ctx, mod_w, mod_b, norm1_g, norm2_g, ev_w_in, ev_w_out, sc_conv_w, da_q_norm, da_k_norm,
           da_lq1, da_lk1, da_lq2, da_lk2, da_subln_g, od_w_in, od_w_out, pool_w, pool_scale, ret_decay_f,
           ret_decay_b, ret_gn_g, ffn_w_up, ffn_conv_w, ffn_conv_b, ffn_w_down):
    batch, seq, d = x.shape
    ctx_len = ctx.shape[1]
    depth = mod_w.shape[0]
    t_lat, t_ctx = batch * seq, batch * ctx_len
    xs = x.reshape(t_lat, d)
    cs = ctx.reshape(t_ctx, d)

    pad = (-(batch + 1)) % 8
    c_all = jnp.concatenate([c, c_ctx[None, :], jnp.zeros((pad, d), F32)], axis=0)
    mods = _modulation(c_all, mod_w, mod_b)
    rope = _rope_tables(seq)

    tm_lat = _tile(seq, 512)
    tm_ctx = _tile(t_ctx, 512)
    tm_ffn_lat = _tile(seq, 1024)
    tm_ffn_ctx = _tile(t_ctx, 1024)
    tq = _tile(seq, 256)
    fdim = ffn_w_down.shape[1]
    fc = 256 if fdim % 256 == 0 else LANES

    for i in range(depth):
        last = i == depth - 1
        j = i // 2
        mod_l = mods[i, :batch].reshape(batch, 6, d)
        mod_c = mods[i, batch:batch + 1].reshape(1, 6, d)
        w_up = ffn_w_up[i].astype(BF16)
        w_down = ffn_w_down[i].astype(BF16)
        if i % 2 == 0:
            w_in = ev_w_in[j].astype(BF16)
            w_out = ev_w_out[j].astype(BF16)
            scw = ev_w_out.shape[1] // 2
            qkw = (w_in.shape[1] - 4 * scw) // 2
            q_scale = ROPE_HEAD_DIM ** -0.5 * LOG2E
            segs = [(0, 3 * scw, "copy", 0, 1.0),
                    (3 * scw, 3 * scw + qkw, "qknorm", 0, q_scale),
                    (3 * scw + qkw, 3 * scw + 2 * qkw, "qknorm", 1, 1.0),
                    (3 * scw + 2 * qkw, w_in.shape[1], "copy", 0, 1.0)]
            dts = [F32, BF16, BF16, BF16]
            gains = [da_q_norm[j], da_k_norm[j]]
            lam_init = 0.8 - 0.6 * math.exp(-0.3 * i)
            lam_vecs = jnp.stack([da_lq1[j], da_lk1[j], da_lq2[j], da_lk2[j]], axis=0)
            bound = (ROPE_HEAD_DIM * q_scale * BOUND_MARGIN * jnp.max(jnp.abs(da_q_norm[j]))
                     * jnp.max(jnp.abs(da_k_norm[j]))).reshape(1).astype(F32)
            g_l, q_l, k_l, v_l = _in_proj(xs, mod_l, norm1_g[i], w_in, segs, dts, gains, rope, seq, tm_lat,
                                          "in_even_lat")
            g_c, q_c, k_c, v_c = _in_proj(cs, mod_c, norm1_g[i], w_in, segs, dts, gains, None, ctx_len, tm_ctx,
                                          "in_even_ctx")
            o_l = _diff_attn(bound, lam_vecs,q_l, [(k_l, v_l, seq), (k_c, v_c, ctx_len)], seq, tq, lam_init, "attn_lat")
            xs_new = _out_even(xs, g_l, o_l, mod_l, w_out, sc_conv_w[j], da_subln_g[j], seq, tm_lat,
                               1.0 - lam_init, "out_even_lat")
            if not last:
                o_c = _diff_attn(bound, lam_vecs,q_c, [(k_c, v_c, ctx_len)], ctx_len, ctx_len, lam_init, "attn_ctx")
                cs = _out_even(cs, g_c, o_c, mod_c, w_out, sc_conv_w[j], da_subln_g[j], ctx_len, tm_ctx,
                               1.0 - lam_init, "out_even_ctx")
            xs = xs_new
        else:
            assert last, "an odd layer that still updates the context stream is not implemented"
            w_in = od_w_in[j].astype(BF16)
            w_out = od_w_out[j].astype(BF16)
            pw = pool_scale.shape[1]
            vw = HEADS * HEAD_V
            qw = (w_in.shape[1] - pw - 2 * vw) // 2
            segs = [(0, pw, "copy", 0, 1.0),
                    (pw, pw + qw, "rope", 0, ROPE_HEAD_DIM ** -0.5),
                    (pw + qw, pw + 2 * qw, "rope", 0, 1.0),
                    (pw + 2 * qw, pw + 2 * qw + vw, "copy", 0, 1.0),
                    (pw + 2 * qw + vw, w_in.shape[1], "copy", 0, 1.0)]
            pv, q_l, k_l, v_l, gate = _in_proj(xs, mod_l, norm1_g[i], w_in, segs, [F32] * 5, [], rope, seq, tm_lat,
                                               "in_odd_lat")
            w_kv = w_in[:, pw + qw:pw + 2 * qw + vw]
            k_c, v_c = _in_proj(cs, mod_c, norm1_g[i], w_kv, [(0, qw, "copy", 0, 1.0), (qw, qw + vw, "copy", 0, 1.0)],
                                [F32, F32], [], None, ctx_len, tm_ctx, "in_odd_ctx")
            o_l = _retention(ret_decay_f[j], ret_decay_b[j], q_l, k_l, v_l, k_c, v_c, seq, ctx_len, "retention")
            xs = _out_odd(xs, pv, gate, o_l, mod_l, w_out, pool_w[j].astype(BF16), pool_scale[j], ret_gn_g[j],
                          seq, tm_lat, "out_odd_lat")
        xs = _ffn(xs, mod_l, norm2_g[i], w_up, ffn_conv_w[i], ffn_conv_b[i], w_down, seq, tm_ffn_lat, fc, "ffn_lat")
        if not last:
            cs = _ffn(cs, mod_c, norm2_g[i], w_up, ffn_conv_w[i], ffn_conv_b[i], w_down, ctx_len, tm_ffn_ctx, fc,
                      "ffn_ctx")
    return xs.reshape(batch, seq, d)
```

```python
import functools
import math

import jax
import jax.numpy as jnp
from jax import lax
from jax.experimental import pallas as pl
from jax.experimental.pallas import tpu as pltpu

F32 = jnp.float32
BF16 = jnp.bfloat16

EPS = 1e-6
GRID_W = 64
ROPE_HEAD_DIM = 64
ROPE_FREQS = ROPE_HEAD_DIM // 4
ROPE_BASE = 10000.0
HEADS = 4
HEAD_V = 128
RET_CHUNK = 128
POOL_RADII = (1, 2, 4, 8)
LANES = 128
HALO = 16
VMEM_LIMIT = 56 * 1024 * 1024
LOG2E = math.log2(math.e)
BOUND_MARGIN = 1.02
MAX_SAFE_SCORE_BOUND = 40.0
ATTN_SUB_ROWS = 512


def _params(sem, vmem=VMEM_LIMIT):
    return pltpu.CompilerParams(dimension_semantics=sem, vmem_limit_bytes=vmem)


def _sigmoid(x):
    return 1.0 / (1.0 + jnp.exp(-x))


def _dot(a, b):
    return jnp.dot(a, b, preferred_element_type=F32)


def _dot_nt(a, b):
    return lax.dot_general(a, b, (((1,), (1,)), ((), ())), preferred_element_type=F32)


def _dot_tn(a, b):
    return lax.dot_general(a, b, (((0,), (0,)), ((), ())), preferred_element_type=F32)


def _rms(x, width):
    return x * lax.rsqrt(jnp.sum(x * x, axis=-1, keepdims=True) * (1.0 / width) + EPS)


def _norm_mod(x, g, shift, scale):
    return (_rms(x, x.shape[-1]) * g) * (1.0 + scale) + shift


def _half_mask(shape):
    return lax.broadcasted_iota(jnp.int32, shape, len(shape) - 1) < (LANES // 2)


def _rope(t, cos, sin):
    lane = lax.broadcasted_iota(jnp.int32, t.shape, 1)
    first = lax.rem(lane, 2 * ROPE_FREQS) < ROPE_FREQS
    partner = jnp.where(first, pltpu.roll(t, LANES - ROPE_FREQS, axis=1), pltpu.roll(t, ROPE_FREQS, axis=1))
    return t * cos + partner * sin


def _norm64(t, gain):
    lo = _half_mask(t.shape)
    sq = t * t
    s_lo = jnp.sum(jnp.where(lo, sq, 0.0), axis=-1, keepdims=True)
    s_hi = jnp.sum(jnp.where(lo, 0.0, sq), axis=-1, keepdims=True)
    w = 1.0 / (LANES // 2)
    r = jnp.where(lo, lax.rsqrt(s_lo * w + EPS), lax.rsqrt(s_hi * w + EPS))
    return (t * r) * gain


def _mod_kernel(c_ref, w_ref, b_ref, o_ref):
    c = c_ref[...]
    a = c * _sigmoid(c)
    o_ref[...] = jnp.dot(a, w_ref[...], preferred_element_type=F32,
                         precision=lax.Precision.HIGHEST) + b_ref[...]


def _modulation(c_all, mod_w, mod_b):
    depth, d, n6 = mod_w.shape
    rows = c_all.shape[0]
    tn = 1536
    return pl.pallas_call(
        _mod_kernel,
        grid=(depth, n6 // tn),
        in_specs=[
            pl.BlockSpec((rows, d), lambda l, j: (0, 0)),
            pl.BlockSpec((None, d, tn), lambda l, j: (l, 0, j)),
            pl.BlockSpec((None, 1, tn), lambda l, j: (l, 0, j)),
        ],
        out_specs=pl.BlockSpec((None, rows, tn), lambda l, j: (l, 0, j)),
        out_shape=jax.ShapeDtypeStruct((depth, rows, n6), F32),
        compiler_params=_params(("parallel", "parallel")),
        name="modulation",
    )(c_all, mod_w, mod_b.reshape(depth, 1, n6))


def _in_proj_kernel(*refs, segs, rotate, n_gain):
    x_ref, mod_ref, g_ref, w_ref = refs[:4]
    gain_refs = refs[4:4 + n_gain]
    pos = 4 + n_gain
    if rotate:
        cos_ref, sin_ref = refs[pos:pos + 2]
        pos += 2
    out_refs = refs[pos:]
    h = _norm_mod(x_ref[...], g_ref[...], mod_ref[0:1, :], mod_ref[1:2, :]).astype(BF16)
    for (c0, c1, mode, gain_idx, scale), o_ref in zip(segs, out_refs):
        y = _dot(h, w_ref[:, c0:c1])
        if mode == "copy":
            o_ref[...] = y.astype(o_ref.dtype)
            continue
        for s in range((c1 - c0) // LANES):
            t = y[:, s * LANES:(s + 1) * LANES]
            if mode == "qknorm":
                t = _norm64(t, gain_refs[gain_idx][...])
            if rotate:
                t = _rope(t, cos_ref[...], sin_ref[...])
            if scale != 1.0:
                t = t * scale
            o_ref[:, s * LANES:(s + 1) * LANES] = t.astype(o_ref.dtype)


def _in_proj(x, mod, g, w, segs, out_dtypes, gains, rope, seq_len, tm, name):
    t_rows, d = x.shape
    per_seq = mod.shape[0] > 1
    tiles_per_seq = max(seq_len // tm, 1)
    if per_seq:
        assert seq_len % tm == 0
        mod_map = lambda i: (i // tiles_per_seq, 0, 0)
    else:
        mod_map = lambda i: (0, 0, 0)
    in_specs = [
        pl.BlockSpec((tm, d), lambda i: (i, 0)),
        pl.BlockSpec((None, 6, d), mod_map),
        pl.BlockSpec((1, d), lambda i: (0, 0)),
        pl.BlockSpec(w.shape, lambda i: (0, 0)),
    ]
    args = [x, mod, g.reshape(1, d), w]
    for gn in gains:
        in_specs.append(pl.BlockSpec((1, LANES), lambda i: (0, 0)))
        args.append(jnp.tile(gn.reshape(1, -1), (1, LANES // gn.shape[-1])))
    if rope is not None:
        assert seq_len % tm == 0
        for tab in rope:
            in_specs.append(pl.BlockSpec((tm, LANES), lambda i: (i % tiles_per_seq, 0)))
            args.append(tab)
    out_specs = [pl.BlockSpec((tm, c1 - c0), lambda i: (i, 0)) for (c0, c1, _, _, _) in segs]
    out_shape = [jax.ShapeDtypeStruct((t_rows, c1 - c0), dt) for (c0, c1, _, _, _), dt in zip(segs, out_dtypes)]
    return pl.pallas_call(
        functools.partial(_in_proj_kernel, segs=tuple(segs), rotate=rope is not None, n_gain=len(gains)),
        grid=(t_rows // tm,),
        in_specs=in_specs,
        out_specs=out_specs,
        out_shape=out_shape,
        compiler_params=_params(("parallel",)),
        name=name,
    )(*args)


def _diff_attn_kernel(*refs, n_src, lam_init, sub_rows):
    bound_ref, lam_ref, q_ref = refs[:3]
    kv_refs = refs[3:3 + 2 * n_src]
    o_ref = refs[3 + 2 * n_src]
    lv = lam_ref[...]
    lam = (jnp.exp(jnp.sum(lv[0:1] * lv[1:2], axis=-1, keepdims=True))
           - jnp.exp(jnp.sum(lv[2:3] * lv[3:4], axis=-1, keepdims=True)) + lam_init)
    bound = bound_ref[0]

    def attend(use_bound):
        k0, k1, vs = [], [], []
        for j in range(n_src):
            k = kv_refs[2 * j][...]
            lo = _half_mask(k.shape)
            zero = jnp.zeros_like(k)
            k0.append(jnp.where(lo, k, zero))
            k1.append(jnp.where(lo, zero, k))
            vs.append(kv_refs[2 * j + 1][...])
        for r in range(q_ref.shape[0] // sub_rows):
            attend_rows(slice(r * sub_rows, (r + 1) * sub_rows), k0, k1, vs, use_bound)

    def attend_rows(rows, k0, k1, vs, use_bound):
        q = q_ref[rows, :]

        def softmax_av(ks):
            ss = [_dot_nt(k, q) for k in ks]
            if use_bound:
                m = bound
            else:
                m = functools.reduce(jnp.maximum, [jnp.max(s, axis=0, keepdims=True) for s in ss])
            ps = [jnp.exp2(s - m) for s in ss]
            l = functools.reduce(jnp.add, [jnp.sum(p, axis=0, keepdims=True) for p in ps])
            acc = functools.reduce(jnp.add, [_dot_tn(v, p.astype(BF16)) for v, p in zip(vs, ps)])
            return acc, l

        acc0, l0 = softmax_av(k0)
        acc1, l1 = softmax_av(k1)
        o_t = acc0 * (1.0 / l0) - acc1 * (lam / l1)
        o_ref[rows, :] = o_t.T.astype(o_ref.dtype)

    safe = bound <= MAX_SAFE_SCORE_BOUND

    @pl.when(safe)
    def _():
        attend(True)

    @pl.when(jnp.logical_not(safe))
    def _():
        attend(False)


def _diff_attn(bound, lam_vecs, q, kvs, seq_q, tq, lam_init, name):
    t_rows, width = q.shape
    batch = t_rows // seq_q
    nq = seq_q // tq
    in_specs = [
        pl.BlockSpec(memory_space=pltpu.SMEM),
        pl.BlockSpec(lam_vecs.shape, lambda b, h, i: (0, 0)),
        pl.BlockSpec((tq, HEAD_V), lambda b, h, i: (b * nq + i, h)),
    ]
    args = [bound, lam_vecs, q]
    for k, v, seq_k in kvs:
        in_specs.append(pl.BlockSpec((seq_k, HEAD_V), lambda b, h, i: (b, h)))
        in_specs.append(pl.BlockSpec((seq_k, HEAD_V), lambda b, h, i: (b, h)))
        args += [k, v]
    return pl.pallas_call(
        functools.partial(_diff_attn_kernel, n_src=len(kvs), lam_init=lam_init, sub_rows=min(tq, ATTN_SUB_ROWS)),
        grid=(batch, width // HEAD_V, nq),
        in_specs=in_specs,
        out_specs=pl.BlockSpec((tq, HEAD_V), lambda b, h, i: (b * nq + i, h)),
        out_shape=jax.ShapeDtypeStruct((t_rows, width), BF16),
        compiler_params=_params(("parallel", "parallel", "parallel")),
        name=name,
    )(*args)


def _halo_specs(tm, halo, width, t_rows, col=0):
    per = tm // halo
    last = t_rows // halo - 1
    prev = pl.BlockSpec((halo, width), lambda i, *_: (jnp.maximum(i * per - 1, 0), col))
    nxt = pl.BlockSpec((halo, width), lambda i, *_: (jnp.minimum((i + 1) * per, last), col))
    return prev, nxt


def _seq_pos(tm, seq_len):
    row = lax.broadcasted_iota(jnp.int32, (tm, 1), 0)
    return lax.rem(pl.program_id(0) * tm + row, seq_len)


def _shift_rows(u, u_prev, u_next, pos, seq_len):
    tm = u.shape[0]
    row = lax.broadcasted_iota(jnp.int32, (tm, 1), 0)
    dn = jnp.where(row == 0, u_prev, pltpu.roll(u, 1, axis=0))
    up = jnp.where(row == tm - 1, u_next, pltpu.roll(u, tm - 1, axis=0))
    dn = jnp.where(pos == 0, 0.0, dn)
    up = jnp.where(pos == seq_len - 1, 0.0, up)
    return dn, up


def _head_rms(o, gain, post_scale=1.0):
    parts = []
    for h in range(o.shape[-1] // HEAD_V):
        t = _rms(o[:, h * HEAD_V:(h + 1) * HEAD_V].astype(F32), HEAD_V) * gain
        parts.append(t * post_scale if post_scale != 1.0 else t)
    return parts


def _out_even_kernel(x_ref, gt_ref, gp_ref, gn_ref, o_ref, mod_ref, w_ref, cw_ref, sg_ref, y_ref, *,
                     seq_len, sc_width, post_scale):
    tm = x_ref.shape[0]
    pos = _seq_pos(tm, seq_len)
    w = sc_width
    b_g = gt_ref[:, 0:w].astype(F32)
    u = gt_ref[:, w:2 * w].astype(F32) * gt_ref[:, 2 * w:3 * w].astype(F32)
    gp = gp_ref[...].astype(F32)
    gn = gn_ref[...].astype(F32)
    u_prev = gp[HALO - 1:HALO, w:2 * w] * gp[HALO - 1:HALO, 2 * w:3 * w]
    u_next = gn[0:1, w:2 * w] * gn[0:1, 2 * w:3 * w]
    dn, up = _shift_rows(u, u_prev, u_next, pos, seq_len)
    y_conv = b_g * (dn * cw_ref[0:1, :] + u * cw_ref[1:2, :] + up * cw_ref[2:3, :])
    parts = [y_conv.astype(BF16)] + [p.astype(BF16) for p in _head_rms(o_ref[...], sg_ref[...], post_scale)]
    z = jnp.concatenate(parts, axis=-1)
    y_ref[...] = x_ref[...] + mod_ref[2:3, :] * _dot(z, w_ref[...])


def _out_even(x, gates, o, mod, w_out, conv_w, subln_g, seq_len, tm, post_scale, name):
    t_rows, d = x.shape
    gw = gates.shape[1]
    per_seq = mod.shape[0] > 1
    tiles_per_seq = max(seq_len // tm, 1)
    mod_map = (lambda i: (i // tiles_per_seq, 0, 0)) if per_seq else (lambda i: (0, 0, 0))
    prev, nxt = _halo_specs(tm, HALO, gw, t_rows)
    return pl.pallas_call(
        functools.partial(_out_even_kernel, seq_len=seq_len, sc_width=gw // 3, post_scale=post_scale),
        grid=(t_rows // tm,),
        in_specs=[
            pl.BlockSpec((tm, d), lambda i: (i, 0)),
            pl.BlockSpec((tm, gw), lambda i: (i, 0)),
            prev, nxt,
            pl.BlockSpec((tm, o.shape[1]), lambda i: (i, 0)),
            pl.BlockSpec((None, 6, d), mod_map),
            pl.BlockSpec(w_out.shape, lambda i: (0, 0)),
            pl.BlockSpec(conv_w.shape, lambda i: (0, 0)),
            pl.BlockSpec((1, HEAD_V), lambda i: (0, 0)),
        ],
        out_specs=pl.BlockSpec((tm, d), lambda i: (i, 0)),
        out_shape=jax.ShapeDtypeStruct((t_rows, d), F32),
        compiler_params=_params(("parallel",)),
        name=name,
    )(x, gates, gates, gates, o, mod, w_out, conv_w, subln_g.reshape(1, HEAD_V))


def _ffn_kernel(x_ref, xp_ref, xn_ref, mod_ref, g_ref, wa_ref, wb_ref, cw_ref, cb_ref, wd_ref, y_ref, h_ref, *,
                seq_len):
    tm = x_ref.shape[0]
    nf, _, fc = wa_ref.shape
    multi_seq = tm > seq_len

    def nm(v):
        return _norm_mod(v, g_ref[...], mod_ref[3:4, :], mod_ref[4:5, :])
    if multi_seq:
        keep_prev = keep_next = 1.0
        pos = _seq_pos(tm, seq_len)
    else:
        r0 = lax.rem(pl.program_id(0) * tm, seq_len)
        keep_prev = (r0 != 0).astype(F32)
        keep_next = (r0 + tm != seq_len).astype(F32)
    h_ref[0:HALO, :] = (nm(xp_ref[...]) * keep_prev).astype(BF16)
    h_ref[HALO:HALO + tm, :] = nm(x_ref[...]).astype(BF16)
    h_ref[HALO + tm:2 * HALO + tm, :] = (nm(xn_ref[...]) * keep_next).astype(BF16)

    rows = tm + 2 * HALO
    acc = None
    for j in range(nf):
        cols = slice(j * fc, (j + 1) * fc)
        a_ext = _dot(h_ref[...], wa_ref[j])
        dn = pltpu.roll(a_ext, 1, axis=0)[HALO:HALO + tm]
        up = pltpu.roll(a_ext, rows - 1, axis=0)[HALO:HALO + tm]
        if multi_seq:
            dn = jnp.where(pos == 0, 0.0, dn)
            up = jnp.where(pos == seq_len - 1, 0.0, up)
        a = (dn * cw_ref[0:1, cols] + a_ext[HALO:HALO + tm] * cw_ref[1:2, cols] + up * cw_ref[2:3, cols]
             + cb_ref[:, cols])
        b = _dot(h_ref[HALO:HALO + tm, :], wb_ref[j])
        part = _dot((a * _sigmoid(a) * b).astype(BF16), wd_ref[j])
        acc = part if acc is None else acc + part
    y_ref[...] = x_ref[...] + mod_ref[5:6, :] * acc


def _ffn(x, mod, g, w_a, w_b, conv_w, conv_b, w_d, seq_len, tm, name):
    t_rows, d = x.shape
    nf, _, fc = w_a.shape
    per_seq = mod.shape[0] > 1
    tiles_per_seq = max(seq_len // tm, 1)
    mod_map = (lambda i: (i // tiles_per_seq, 0, 0)) if per_seq else (lambda i: (0, 0, 0))
    prev, nxt = _halo_specs(tm, HALO, d, t_rows)

    def resident(shape):
        return pl.BlockSpec(shape, lambda i: (0,) * len(shape), pipeline_mode=pl.Buffered(1))

    return pl.pallas_call(
        functools.partial(_ffn_kernel, seq_len=seq_len),
        grid=(t_rows // tm,),
        in_specs=[
            pl.BlockSpec((tm, d), lambda i: (i, 0)),
            prev, nxt,
            pl.BlockSpec((None, 6, d), mod_map),
            pl.BlockSpec((1, d), lambda i: (0, 0)),
            resident(w_a.shape), resident(w_b.shape),
            pl.BlockSpec((3, nf * fc), lambda i: (0, 0)),
            pl.BlockSpec((1, nf * fc), lambda i: (0, 0)),
            resident(w_d.shape),
        ],
        out_specs=pl.BlockSpec((tm, d), lambda i: (i, 0)),
        out_shape=jax.ShapeDtypeStruct((t_rows, d), F32),
        scratch_shapes=[pltpu.VMEM((tm + 2 * HALO, d), BF16)],
        compiler_params=_params(("parallel",)),
        name=name,
    )(x, x, x, mod, g.reshape(1, d), w_a, w_b, conv_w, conv_b.reshape(1, nf * fc), w_d)


def _retention_kernel(df_ref, db_ref, q_ref, k_ref, v_ref, kc_ref, vc_ref, o_ref, kv_ref, ss_ref):
    n = q_ref.shape[0]
    lc = kc_ref.shape[0]
    c = RET_CHUNK
    nc = n // c
    pos = lax.broadcasted_iota(jnp.int32, (c, 1), 0).astype(F32)
    ii = lax.broadcasted_iota(jnp.int32, (c, c), 0)
    jj = lax.broadcasted_iota(jnp.int32, (c, c), 1)
    dij = (ii - jj).astype(F32)
    jpos = lax.broadcasted_iota(jnp.int32, (lc, 1), 0).astype(F32)
    lo = _half_mask((1, LANES))
    row_lo = lax.broadcasted_iota(jnp.int32, (LANES, 1), 0) < (LANES // 2)
    block_diag = (lax.broadcasted_iota(jnp.int32, (LANES, 2 * HEAD_V), 0) // (LANES // 2)
                  == lax.broadcasted_iota(jnp.int32, (LANES, 2 * HEAD_V), 1) // HEAD_V)
    zeros_v = jnp.zeros((c, HEAD_V), BF16)

    def cat(a, b):
        return jnp.concatenate([a, b], axis=-1)

    for pair in range(HEADS // 2):
        slab = slice(pair * LANES, (pair + 1) * LANES)
        vcol = slice(pair * 2 * HEAD_V, (pair + 1) * 2 * HEAD_V)
        lg = []
        for d_ref in (df_ref, db_ref):
            lg.append([-jnp.exp(jnp.full((1, 1), d_ref[2 * pair + e], F32)) for e in range(2)])
        (lgf0, lgf1), (lgb0, lgb1) = lg
        lgf_lane = jnp.where(lo, lgf0, lgf1)
        lgb_lane = jnp.where(lo, lgb0, lgb1)
        lgf_row = jnp.where(row_lo, lgf0, lgf1)
        lgb_row = jnp.where(row_lo, lgb0, lgb1)
        qdec = cat(jnp.exp(lgf_lane * (pos + 1.0)), jnp.exp(lgb_lane * (c - pos)))
        kdec = cat(jnp.exp(lgf_lane * (c - 1.0 - pos)), jnp.exp(lgb_lane * pos))
        cdec_f = jnp.exp(lgf_row * float(c))
        cdec_b = jnp.exp(lgb_row * float(c))
        intra = [jnp.where(dij >= 0, jnp.exp(f * jnp.maximum(dij, 0.0)), 0.0)
                 + jnp.where(dij <= 0, jnp.exp(b * jnp.maximum(-dij, 0.0)), 0.0)
                 for f, b in ((lgf0, lgb0), (lgf1, lgb1))]

        for ci in range(nc):
            rows = slice(ci * c, (ci + 1) * c)
            k = k_ref[rows, slab].astype(F32)
            kv_ref[ci] = _dot_tn((cat(k, k) * kdec).astype(BF16), v_ref[rows, vcol].astype(BF16))
        kc = kc_ref[:, slab].astype(F32)
        cdec0 = cat(jnp.exp(lgf_lane * (lc - 1.0 - jpos)), jnp.exp(lgb_lane * jpos))
        s0 = _dot_tn((cat(kc, kc) * cdec0).astype(BF16), vc_ref[:, vcol].astype(BF16))

        s = s0[0:LANES]
        for ci in range(nc):
            ss_ref[ci, 0:LANES, :] = jnp.where(block_diag, s, 0.0).astype(BF16)
            s = s * cdec_f + kv_ref[ci, 0:LANES, :]
        s = s0[LANES:2 * LANES]
        for ci in reversed(range(nc)):
            ss_ref[ci, LANES:2 * LANES, :] = jnp.where(block_diag, s, 0.0).astype(BF16)
            s = s * cdec_b + kv_ref[ci, LANES:2 * LANES, :]

        for ci in range(nc):
            rows = slice(ci * c, (ci + 1) * c)
            q = q_ref[rows, slab].astype(F32)
            kb = k_ref[rows, slab].astype(BF16)
            v = v_ref[rows, vcol].astype(BF16)
            att0 = _dot_nt(jnp.where(lo, q, 0.0).astype(BF16), kb) * intra[0]
            att1 = _dot_nt(jnp.where(lo, 0.0, q).astype(BF16), kb) * intra[1]
            v_blocks = jnp.concatenate([cat(v[:, 0:HEAD_V], zeros_v), cat(zeros_v, v[:, HEAD_V:])], axis=0)
            o_ref[rows, vcol] = (_dot(cat(att0, att1).astype(BF16), v_blocks)
                                 + _dot((cat(q, q) * qdec).astype(BF16), ss_ref[ci])).astype(o_ref.dtype)


def _retention(dec_f, dec_b, q, k, v, kc, vc, seq_len, ctx_len, name):
    t_rows = q.shape[0]
    batch = t_rows // seq_len
    qw, vw = q.shape[1], v.shape[1]
    nc = seq_len // RET_CHUNK
    smem = pl.BlockSpec(memory_space=pltpu.SMEM)
    return pl.pallas_call(
        _retention_kernel,
        grid=(batch,),
        in_specs=[
            smem, smem,
            pl.BlockSpec((seq_len, qw), lambda b: (b, 0)),
            pl.BlockSpec((seq_len, qw), lambda b: (b, 0)),
            pl.BlockSpec((seq_len, vw), lambda b: (b, 0)),
            pl.BlockSpec((ctx_len, qw), lambda b: (b, 0)),
            pl.BlockSpec((ctx_len, vw), lambda b: (b, 0)),
        ],
        out_specs=pl.BlockSpec((seq_len, vw), lambda b: (b, 0)),
        out_shape=jax.ShapeDtypeStruct((t_rows, vw), BF16),
        scratch_shapes=[pltpu.VMEM((nc, 2 * LANES, 2 * HEAD_V), F32),
                        pltpu.VMEM((nc, 2 * LANES, 2 * HEAD_V), BF16)],
        compiler_params=_params(("parallel",)),
        name=name,
    )(dec_f, dec_b, q, k, v, kc, vc)


def _out_odd_kernel(x_ref, pv_ref, pp_ref, pn_ref, gt_ref, o_ref, mod_ref, w_ref, pw_ref, ps_ref, gg_ref, y_ref, *,
                    seq_len):
    tm = x_ref.shape[0]
    r0 = lax.rem(pl.program_id(0) * tm, seq_len)
    pos = r0 + lax.broadcasted_iota(jnp.int32, (tm, 1), 0)
    has_prev = (r0 != 0).astype(F32)
    has_next = (r0 + tm != seq_len).astype(F32)
    rows = tm + 2 * HALO
    parts = []
    for gi, r in enumerate(POOL_RADII):
        col = slice(gi * LANES, (gi + 1) * LANES)
        v = pv_ref[:, col].astype(F32)
        ext = jnp.concatenate([pp_ref[:, col].astype(F32) * has_prev, v, pn_ref[:, col].astype(F32) * has_next],
                              axis=0)
        win = ext
        span = 1
        while span < 2 * r:
            win = win + pltpu.roll(win, rows - span, axis=0)
            span *= 2
        total = pltpu.roll(win, r, axis=0) + pltpu.roll(ext, rows - r, axis=0)
        total = total[HALO:HALO + tm]
        cnt = (jnp.minimum(pos + r + 1, seq_len) - jnp.maximum(pos - r, 0)).astype(F32)
        y = (total / cnt - v).astype(BF16)
        parts.append(_dot(y, pw_ref[gi]))
    y_p = jnp.concatenate(parts, axis=-1) * ps_ref[...]
    gate = gt_ref[...].astype(F32)
    y_r =jnp.concatenate(_head_rms(o_ref[...], gg_ref[...]), axis=-1) * (gate * _sigmoid(gate))
    z = jnp.concatenate([y_p.astype(BF16), y_r.astype(BF16)], axis=-1)
    y_ref[...] = x_ref[...] + mod_ref[2:3, :] * _dot(z, w_ref[...])


def _out_odd(x, pv, gate, o, mod, w_out, pool_w, pool_scale, gn_g, seq_len, tm, name):
    t_rows, d = x.shape
    pw = pv.shape[1]
    assert seq_len % tm == 0
    tiles_per_seq = seq_len // tm
    prev, nxt = _halo_specs(tm, HALO, pw, t_rows)
    return pl.pallas_call(
        functools.partial(_out_odd_kernel, seq_len=seq_len),
        grid=(t_rows // tm,),
        in_specs=[
            pl.BlockSpec((tm, d), lambda i: (i, 0)),
            pl.BlockSpec((tm, pw), lambda i: (i, 0)),
            prev, nxt,
            pl.BlockSpec((tm, gate.shape[1]), lambda i: (i, 0)),
            pl.BlockSpec((tm, o.shape[1]), lambda i: (i, 0)),
            pl.BlockSpec((None, 6, d), lambda i: (i // tiles_per_seq, 0, 0)),
            pl.BlockSpec(w_out.shape, lambda i: (0, 0)),
            pl.BlockSpec(pool_w.shape, lambda i: (0, 0, 0)),
            pl.BlockSpec((1, pw), lambda i: (0, 0)),
            pl.BlockSpec((1, HEAD_V), lambda i: (0, 0)),
        ],
        out_specs=pl.BlockSpec((tm, d), lambda i: (i, 0)),
        out_shape=jax.ShapeDtypeStruct((t_rows, d), F32),
        compiler_params=_params(("parallel",)),
        name=name,
    )(x, pv, pv, pv, gate, o, mod, w_out, pool_w, pool_scale.reshape(1, pw), gn_g.reshape(1, HEAD_V))


def _rope_tables(seq_len):
    rows = seq_len // GRID_W
    row = jnp.repeat(jnp.arange(rows, dtype=F32), GRID_W)
    col = jnp.tile(jnp.arange(GRID_W, dtype=F32), rows)
    inv = ROPE_BASE ** (-jnp.arange(ROPE_FREQS, dtype=F32) / ROPE_FREQS)
    ar, ac = row[:, None] * inv, col[:, None] * inv
    cos = jnp.concatenate([jnp.cos(ar), jnp.cos(ar), jnp.cos(ac), jnp.cos(ac)], axis=-1)
    sin = jnp.concatenate([-jnp.sin(ar), jnp.sin(ar), -jnp.sin(ac), jnp.sin(ac)], axis=-1)
    reps = LANES // ROPE_HEAD_DIM
    return jnp.tile(cos, (1, reps)), jnp.tile(sin, (1, reps))


def _tile(total, want):
    t = min(want, total)
    assert total % t == 0
    return t


def kernel(x, c, ctx, c_ctx, mod_w, mod_b, norm1_g, norm2_g, ev_w_in, ev_w_out, sc_conv_w, da_q_norm, da_k_norm,
           da_lq1, da_lk1, da_lq2, da_lk2, da_subln_g, od_w_in, od_w_out, pool_w, pool_scale, ret_decay_f,
           ret_decay_b, ret_gn_g, ffn_w_up, ffn_conv_w, ffn_conv_b, ffn_w_down):
    batch, seq, d = x.shape
    ctx_len = ctx.shape[1]
    depth = mod_w.shape[0]
    t_lat, t_ctx = batch * seq, batch * ctx_len
    xs = x.reshape(t_lat, d)
    cs = ctx.reshape(t_ctx, d)

    pad = (-(batch + 1)) % 8
    c_all = jnp.concatenate([c, c_ctx[None, :], jnp.zeros((pad, d), F32)], axis=0)
    mods = _modulation(c_all, mod_w, mod_b)
    rope = _rope_tables(seq)

    tm_lat = _tile(seq, 512)
    tm_ctx = _tile(t_ctx, 512)
    tm_ffn_lat = _tile(seq, 1024)
    tm_ffn_ctx = _tile(t_ctx, 1024)
    tq = _tile(seq, ATTN_SUB_ROWS)
    fdim = ffn_w_down.shape[1]
    fc = 256 if fdim % 256 == 0 else LANES

    for i in range(depth):
        last = i == depth - 1
        j = i // 2
        mod_l = mods[i, :batch].reshape(batch, 6, d)
        mod_c = mods[i, batch:batch + 1].reshape(1, 6, d)
        w_up = ffn_w_up[i].astype(BF16).reshape(d, 2, fdim // fc, fc).transpose(1, 2, 0, 3)
        w_a, w_b = w_up[0], w_up[1]
        w_d = ffn_w_down[i].astype(BF16).reshape(fdim // fc, fc, d)
        if i % 2 == 0:
            w_in = ev_w_in[j].astype(BF16)
            w_out = ev_w_out[j].astype(BF16)
            scw = ev_w_out.shape[1] // 2
            qkw = (w_in.shape[1] - 4 * scw) // 2
            q_scale = ROPE_HEAD_DIM ** -0.5 * LOG2E
            segs = [(3 * scw, 3 * scw + qkw, "qknorm", 0, q_scale),
                    (3 * scw + qkw, 3 * scw + 2 * qkw, "qknorm", 1, 1.0),
                    (0, 3 * scw, "copy", 0, 1.0),
                    (3 * scw + 2 * qkw, w_in.shape[1], "copy", 0, 1.0)]
            dts = [BF16, BF16, BF16, BF16]
            gains = [da_q_norm[j], da_k_norm[j]]
            lam_init = 0.8 - 0.6 * math.exp(-0.3 * i)
            lam_vecs = jnp.stack([da_lq1[j], da_lk1[j], da_lq2[j], da_lk2[j]], axis=0)
            bound = (ROPE_HEAD_DIM * q_scale * BOUND_MARGIN * jnp.max(jnp.abs(da_q_norm[j]))
                     * jnp.max(jnp.abs(da_k_norm[j]))).reshape(1).astype(F32)
            q_l, k_l, g_l, v_l = _in_proj(xs, mod_l, norm1_g[i], w_in, segs, dts, gains, rope, seq, tm_lat,
                                          "in_even_lat")
            q_c, k_c, g_c, v_c = _in_proj(cs, mod_c, norm1_g[i], w_in, segs, dts, gains, None, ctx_len, tm_ctx,
                                          "in_even_ctx")
            o_l = _diff_attn(bound, lam_vecs,q_l, [(k_l, v_l, seq), (k_c, v_c, ctx_len)], seq, tq, lam_init, "attn_lat")
            xs_new = _out_even(xs, g_l, o_l, mod_l, w_out, sc_conv_w[j], da_subln_g[j], seq, tm_lat,
                               1.0 - lam_init, "out_even_lat")
            if not last:
                o_c = _diff_attn(bound, lam_vecs,q_c, [(k_c, v_c, ctx_len)], ctx_len, ctx_len, lam_init, "attn_ctx")
                cs = _out_even(cs, g_c, o_c, mod_c, w_out, sc_conv_w[j], da_subln_g[j], ctx_len, tm_ctx,
                               1.0 - lam_init, "out_even_ctx")
            xs = xs_new
        else:
            assert last, "an odd layer that still updates the context stream is not implemented"
            w_in = od_w_in[j].astype(BF16)
            w_out = od_w_out[j].astype(BF16)
            pw = pool_scale.shape[1]
            vw = HEADS * HEAD_V
            qw = (w_in.shape[1] - pw - 2 * vw) // 2
            segs = [(pw, pw + qw, "rope", 0, ROPE_HEAD_DIM ** -0.5),
                    (pw + qw, pw + 2 * qw, "rope", 0, 1.0),
                    (0, pw, "copy", 0, 1.0),
                    (pw + 2 * qw, pw + 2 * qw + vw, "copy", 0, 1.0),
                    (pw + 2 * qw + vw, w_in.shape[1], "copy", 0, 1.0)]
            q_l, k_l, pv, v_l, gate = _in_proj(xs, mod_l, norm1_g[i], w_in, segs, [BF16] * 5, [], rope, seq, tm_lat,
                                               "in_odd_lat")
            w_kv = w_in[:, pw + qw:pw + 2 * qw + vw]
            k_c, v_c = _in_proj(cs, mod_c, norm1_g[i], w_kv, [(0, qw, "copy", 0, 1.0), (qw, qw + vw, "copy", 0, 1.0)],
                                [BF16, BF16], [], None, ctx_len, tm_ctx, "in_odd_ctx")
            o_l = _retention(ret_decay_f[j], ret_decay_b[j], q_l, k_l, v_l, k_c, v_c, seq, ctx_len, "retention")
            xs = _out_odd(xs, pv, gate, o_l, mod_l, w_out, pool_w[j].astype(BF16), pool_scale[j], ret_gn_g[j],
                          seq, tm_lat, "out_odd_lat")
        xs = _ffn(xs, mod_l, norm2_g[i], w_a, w_b, ffn_conv_w[i], ffn_conv_b[i], w_d, seq, tm_ffn_lat, "ffn_lat")
        if not last:
            cs = _ffn(cs, mod_c, norm2_g[i], w_a, w_b, ffn_conv_w[i], ffn_conv_b[i], w_d, ctx_len, tm_ffn_ctx,
                      "ffn_ctx")
    return xs.reshape(batch, seq, d)
```

```python
import functools
import math

import jax
import jax.numpy as jnp
from jax import lax
from jax.experimental import pallas as pl
from jax.experimental.pallas import tpu as pltpu

F32 = jnp.float32
BF16 = jnp.bfloat16

EPS = 1e-6
GRID_W = 64
ROPE_HEAD_DIM = 64
ROPE_FREQS = ROPE_HEAD_DIM // 4
ROPE_BASE = 10000.0
HEADS = 4
HEAD_V = 128
RET_CHUNK = 128
POOL_RADII = (1, 2, 4, 8)
LANES = 128
HALO = 16
VMEM_LIMIT = 56 * 1024 * 1024
LOG2E = math.log2(math.e)
BOUND_MARGIN = 1.02
MAX_SAFE_SCORE_BOUND = 40.0
ATTN_SUB_ROWS = 512


def _params(sem, vmem=VMEM_LIMIT):
    return pltpu.CompilerParams(dimension_semantics=sem, vmem_limit_bytes=vmem)


def _sigmoid(x):
    return 1.0 / (1.0 + jnp.exp(-x))


def _dot(a, b):
    return jnp.dot(a, b, preferred_element_type=F32)


def _dot_nt(a, b):
    return lax.dot_general(a, b, (((1,), (1,)), ((), ())), preferred_element_type=F32)


def _dot_tn(a, b):
    return lax.dot_general(a, b, (((0,), (0,)), ((), ())), preferred_element_type=F32)


def _rms(x, width):
    return x * lax.rsqrt(jnp.sum(x * x, axis=-1, keepdims=True) * (1.0 / width) + EPS)


def _norm_mod(x, g, shift, scale):
    return (_rms(x, x.shape[-1]) * g) * (1.0 + scale) + shift


def _half_mask(shape):
    return lax.broadcasted_iota(jnp.int32, shape, len(shape) - 1) < (LANES // 2)


def _rope(t, cos, sin):
    lane = lax.broadcasted_iota(jnp.int32, t.shape, 1)
    first = lax.rem(lane, 2 * ROPE_FREQS) < ROPE_FREQS
    partner = jnp.where(first, pltpu.roll(t, LANES - ROPE_FREQS, axis=1), pltpu.roll(t, ROPE_FREQS, axis=1))
    return t * cos + partner * sin


def _norm64(t, gain):
    lo = _half_mask(t.shape)
    sq = t * t
    s_lo = jnp.sum(jnp.where(lo, sq, 0.0), axis=-1, keepdims=True)
    s_hi = jnp.sum(jnp.where(lo, 0.0, sq), axis=-1, keepdims=True)
    w = 1.0 / (LANES // 2)
    r = jnp.where(lo, lax.rsqrt(s_lo * w + EPS), lax.rsqrt(s_hi * w + EPS))
    return (t * r) * gain


def _mod_kernel(c_ref, w_ref, b_ref, o_ref):
    c = c_ref[...]
    a = c * _sigmoid(c)
    o_ref[...] = _dot(a.astype(BF16), w_ref[...].astype(BF16)) + b_ref[...]


def _modulation(c_all, mod_w, mod_b):
    depth, d, n6 = mod_w.shape
    rows = c_all.shape[0]
    tn = 1536
    return pl.pallas_call(
        _mod_kernel,
        grid=(depth, n6 // tn),
        in_specs=[
            pl.BlockSpec((rows, d), lambda l, j: (0, 0)),
            pl.BlockSpec((None, d, tn), lambda l, j: (l, 0, j)),
            pl.BlockSpec((None, 1, tn), lambda l, j: (l, 0, j)),
        ],
        out_specs=pl.BlockSpec((None, rows, tn), lambda l, j: (l, 0, j)),
        out_shape=jax.ShapeDtypeStruct((depth, rows, n6), F32),
        compiler_params=_params(("parallel", "parallel")),
        name="modulation",
    )(c_all, mod_w, mod_b.reshape(depth, 1, n6))


def _in_proj_kernel(*refs, segs, rotate, n_gain):
    x_ref, mod_ref, g_ref, w_ref = refs[:4]
    gain_refs = refs[4:4 + n_gain]
    pos = 4 + n_gain
    if rotate:
        cos_ref, sin_ref = refs[pos:pos + 2]
        pos += 2
    out_refs = refs[pos:]
    h = _norm_mod(x_ref[...], g_ref[...], mod_ref[0:1, :], mod_ref[1:2, :]).astype(BF16)
    for (c0, c1, mode, gain_idx, scale), o_ref in zip(segs, out_refs):
        y = _dot(h, w_ref[:, c0:c1])
        if mode == "copy":
            o_ref[...] = y.astype(o_ref.dtype)
            continue
        for s in range((c1 - c0) // LANES):
            t = y[:, s * LANES:(s + 1) * LANES]
            if mode == "qknorm":
                t = _norm64(t, gain_refs[gain_idx][...])
            if rotate:
                t = _rope(t, cos_ref[...], sin_ref[...])
            if scale != 1.0:
                t = t * scale
            o_ref[:, s * LANES:(s + 1) * LANES] = t.astype(o_ref.dtype)


def _in_proj(x, mod, g, w, segs, out_dtypes, gains, rope, seq_len, tm, name):
    t_rows, d = x.shape
    per_seq = mod.shape[0] > 1
    tiles_per_seq = max(seq_len // tm, 1)
    if per_seq:
        assert seq_len % tm == 0
        mod_map = lambda i: (i // tiles_per_seq, 0, 0)
    else:
        mod_map = lambda i: (0, 0, 0)
    in_specs = [
        pl.BlockSpec((tm, d), lambda i: (i, 0)),
        pl.BlockSpec((None, 6, d), mod_map),
        pl.BlockSpec((1, d), lambda i: (0, 0)),
        pl.BlockSpec(w.shape, lambda i: (0, 0)),
    ]
    args = [x, mod, g.reshape(1, d), w]
    for gn in gains:
        in_specs.append(pl.BlockSpec((1, LANES), lambda i: (0, 0)))
        args.append(jnp.tile(gn.reshape(1, -1), (1, LANES // gn.shape[-1])))
    if rope is not None:
        assert seq_len % tm == 0
        for tab in rope:
            in_specs.append(pl.BlockSpec((tm, LANES), lambda i: (i % tiles_per_seq, 0)))
            args.append(tab)
    out_specs = [pl.BlockSpec((tm, c1 - c0), lambda i: (i, 0)) for (c0, c1, _, _, _) in segs]
    out_shape = [jax.ShapeDtypeStruct((t_rows, c1 - c0), dt) for (c0, c1, _, _, _), dt in zip(segs, out_dtypes)]
    return pl.pallas_call(
        functools.partial(_in_proj_kernel, segs=tuple(segs), rotate=rope is not None, n_gain=len(gains)),
        grid=(t_rows // tm,),
        in_specs=in_specs,
        out_specs=out_specs,
        out_shape=out_shape,
        compiler_params=_params(("parallel",)),
        name=name,
    )(*args)


def _diff_attn_kernel(*refs, n_src, lam_init, sub_rows):
    bound_ref, lam_ref, q_ref = refs[:3]
    kv_refs = refs[3:3 + 2 * n_src]
    o_ref = refs[3 + 2 * n_src]
    lv = lam_ref[...]
    lam = (jnp.exp(jnp.sum(lv[0:1] * lv[1:2], axis=-1, keepdims=True))
           - jnp.exp(jnp.sum(lv[2:3] * lv[3:4], axis=-1, keepdims=True)) + lam_init)
    bound = bound_ref[0]

    def attend(use_bound):
        k0, k1, vs = [], [], []
        for j in range(n_src):
            k = kv_refs[2 * j][...]
            lo = _half_mask(k.shape)
            zero = jnp.zeros_like(k)
            k0.append(jnp.where(lo, k, zero))
            k1.append(jnp.where(lo, zero, k))
            vs.append(kv_refs[2 * j + 1][...])
        for r in range(q_ref.shape[0] // sub_rows):
            attend_rows(slice(r * sub_rows, (r + 1) * sub_rows), k0, k1, vs, use_bound)

    def attend_rows(rows, k0, k1, vs, use_bound):
        q = q_ref[rows, :]

        def softmax_av(ks):
            ss = [_dot_nt(k, q) for k in ks]
            if use_bound:
                m = bound
            else:
                m = functools.reduce(jnp.maximum, [jnp.max(s, axis=0, keepdims=True) for s in ss])
            ps = [jnp.exp2(s - m) for s in ss]
            l = functools.reduce(jnp.add, [jnp.sum(p, axis=0, keepdims=True) for p in ps])
            acc = functools.reduce(jnp.add, [_dot_tn(v, p.astype(BF16)) for v, p in zip(vs, ps)])
            return acc, l

        acc0, l0 = softmax_av(k0)
        acc1, l1 = softmax_av(k1)
        o_t = acc0 * (1.0 / l0) - acc1 * (lam / l1)
        o_ref[rows, :] = o_t.T.astype(o_ref.dtype)

    safe = bound <= MAX_SAFE_SCORE_BOUND

    @pl.when(safe)
    def _():
        attend(True)

    @pl.when(jnp.logical_not(safe))
    def _():
        attend(False)


def _diff_attn(bound, lam_vecs, q, kvs, seq_q, tq, lam_init, name):
    t_rows, width = q.shape
    batch = t_rows // seq_q
    nq = seq_q // tq
    in_specs = [
        pl.BlockSpec(memory_space=pltpu.SMEM),
        pl.BlockSpec(lam_vecs.shape, lambda b, h, i: (0, 0)),
        pl.BlockSpec((tq, HEAD_V), lambda b, h, i: (b * nq + i, h)),
    ]
    args = [bound, lam_vecs, q]
    for k, v, seq_k in kvs:
        in_specs.append(pl.BlockSpec((seq_k, HEAD_V), lambda b, h, i: (b, h)))
        in_specs.append(pl.BlockSpec((seq_k, HEAD_V), lambda b, h, i: (b, h)))
        args += [k, v]
    return pl.pallas_call(
        functools.partial(_diff_attn_kernel, n_src=len(kvs), lam_init=lam_init, sub_rows=min(tq, ATTN_SUB_ROWS)),
        grid=(batch, width // HEAD_V, nq),
        in_specs=in_specs,
        out_specs=pl.BlockSpec((tq, HEAD_V), lambda b, h, i: (b * nq + i, h)),
        out_shape=jax.ShapeDtypeStruct((t_rows, width), BF16),
        compiler_params=_params(("parallel", "parallel", "parallel")),
        name=name,
    )(*args)


def _halo_specs(tm, halo, width, t_rows, col=0):
    per = tm // halo
    last = t_rows // halo - 1
    prev = pl.BlockSpec((halo, width), lambda i, *_: (jnp.maximum(i * per - 1, 0), col))
    nxt = pl.BlockSpec((halo, width), lambda i, *_: (jnp.minimum((i + 1) * per, last), col))
    return prev, nxt


def _seq_pos(tm, seq_len):
    row = lax.broadcasted_iota(jnp.int32, (tm, 1), 0)
    return lax.rem(pl.program_id(0) * tm + row, seq_len)


def _shift_rows(u, u_prev, u_next, pos, seq_len):
    tm = u.shape[0]
    row = lax.broadcasted_iota(jnp.int32, (tm, 1), 0)
    dn = jnp.where(row == 0, u_prev, pltpu.roll(u, 1, axis=0))
    up = jnp.where(row == tm - 1, u_next, pltpu.roll(u, tm - 1, axis=0))
    dn = jnp.where(pos == 0, 0.0, dn)
    up = jnp.where(pos == seq_len - 1, 0.0, up)
    return dn, up


def _head_rms(o, gain, post_scale=1.0):
    parts = []
    for h in range(o.shape[-1] // HEAD_V):
        t = _rms(o[:, h * HEAD_V:(h + 1) * HEAD_V].astype(F32), HEAD_V) * gain
        parts.append(t * post_scale if post_scale != 1.0 else t)
    return parts


def _out_even_kernel(x_ref, gt_ref, gp_ref, gn_ref, o_ref, mod_ref, w_ref, cw_ref, sg_ref, y_ref, *,
                     seq_len, sc_width, post_scale):
    tm = x_ref.shape[0]
    pos = _seq_pos(tm, seq_len)
    w = sc_width
    b_g = gt_ref[:, 0:w].astype(F32)
    u = gt_ref[:, w:2 * w].astype(F32) * gt_ref[:, 2 * w:3 * w].astype(F32)
    gp = gp_ref[...].astype(F32)
    gn = gn_ref[...].astype(F32)
    u_prev = gp[HALO - 1:HALO, w:2 * w] * gp[HALO - 1:HALO, 2 * w:3 * w]
    u_next = gn[0:1, w:2 * w] * gn[0:1, 2 * w:3 * w]
    dn, up = _shift_rows(u, u_prev, u_next, pos, seq_len)
    y_conv = b_g * (dn * cw_ref[0:1, :] + u * cw_ref[1:2, :] + up * cw_ref[2:3, :])
    parts = [y_conv.astype(BF16)] + [p.astype(BF16) for p in _head_rms(o_ref[...], sg_ref[...], post_scale)]
    z = jnp.concatenate(parts, axis=-1)
    y_ref[...] = x_ref[...] + mod_ref[2:3, :] * _dot(z, w_ref[...])


def _out_even(x, gates, o, mod, w_out, conv_w, subln_g, seq_len, tm, post_scale, name):
    t_rows, d = x.shape
    gw = gates.shape[1]
    per_seq = mod.shape[0] > 1
    tiles_per_seq = max(seq_len // tm, 1)
    mod_map = (lambda i: (i // tiles_per_seq, 0, 0)) if per_seq else (lambda i: (0, 0, 0))
    prev, nxt = _halo_specs(tm, HALO, gw, t_rows)
    return pl.pallas_call(
        functools.partial(_out_even_kernel, seq_len=seq_len, sc_width=gw // 3, post_scale=post_scale),
        grid=(t_rows // tm,),
        in_specs=[
            pl.BlockSpec((tm, d), lambda i: (i, 0)),
            pl.BlockSpec((tm, gw), lambda i: (i, 0)),
            prev, nxt,
            pl.BlockSpec((tm, o.shape[1]), lambda i: (i, 0)),
            pl.BlockSpec((None, 6, d), mod_map),
            pl.BlockSpec(w_out.shape, lambda i: (0, 0)),
            pl.BlockSpec(conv_w.shape, lambda i: (0, 0)),
            pl.BlockSpec((1, HEAD_V), lambda i: (0, 0)),
        ],
        out_specs=pl.BlockSpec((tm, d), lambda i: (i, 0)),
        out_shape=jax.ShapeDtypeStruct((t_rows, d), F32),
        compiler_params=_params(("parallel",)),
        name=name,
    )(x, gates, gates, gates, o, mod, w_out, conv_w, subln_g.reshape(1, HEAD_V))


def _ffn_kernel(x_ref, xp_ref, xn_ref, mod_ref, g_ref, wu_ref, cw_ref, cb_ref, wd_ref, y_ref, h_ref, act_ref, *,
                seq_len, fc):
    tm = x_ref.shape[0]
    fdim = wd_ref.shape[0]
    multi_seq = tm > seq_len

    def nm(v):
        return _norm_mod(v, g_ref[...], mod_ref[3:4, :], mod_ref[4:5, :])
    if multi_seq:
        keep_prev = keep_next = 1.0
        pos = _seq_pos(tm, seq_len)
    else:
        r0 = lax.rem(pl.program_id(0) * tm, seq_len)
        keep_prev = (r0 != 0).astype(F32)
        keep_next = (r0 + tm != seq_len).astype(F32)
    h_ref[0:HALO, :] = (nm(xp_ref[...]) * keep_prev).astype(BF16)
    h_ref[HALO:HALO + tm, :] = nm(x_ref[...]).astype(BF16)
    h_ref[HALO + tm:2 * HALO + tm, :] = (nm(xn_ref[...]) * keep_next).astype(BF16)

    rows = tm + 2 * HALO
    for j in range(fdim // fc):
        cols = slice(j * fc, (j + 1) * fc)
        a_ext = _dot(h_ref[...], wu_ref[:, cols])
        dn = pltpu.roll(a_ext, 1, axis=0)[HALO:HALO + tm]
        up = pltpu.roll(a_ext, rows - 1, axis=0)[HALO:HALO + tm]
        if multi_seq:
            dn = jnp.where(pos == 0, 0.0, dn)
            up = jnp.where(pos == seq_len - 1, 0.0, up)
        a = (dn * cw_ref[0:1, cols] + a_ext[HALO:HALO + tm] * cw_ref[1:2, cols] + up * cw_ref[2:3, cols]
             + cb_ref[:, cols])
        b = _dot(h_ref[HALO:HALO + tm, :], wu_ref[:, fdim + j * fc:fdim + (j + 1) * fc])
        act_ref[:, cols] = (a * _sigmoid(a) * b).astype(BF16)
    y_ref[...] = x_ref[...] + mod_ref[5:6, :] * _dot(act_ref[...], wd_ref[...])


def _ffn(x, mod, g, w_up, conv_w, conv_b, w_down, seq_len, tm, fc, name):
    t_rows, d = x.shape
    fdim = w_down.shape[0]
    per_seq = mod.shape[0] > 1
    tiles_per_seq = max(seq_len // tm, 1)
    mod_map = (lambda i: (i // tiles_per_seq, 0, 0)) if per_seq else (lambda i: (0, 0, 0))
    prev, nxt = _halo_specs(tm, HALO, d, t_rows)

    def resident(shape):
        return pl.BlockSpec(shape, lambda i: (0,) * len(shape), pipeline_mode=pl.Buffered(1))

    return pl.pallas_call(
        functools.partial(_ffn_kernel, seq_len=seq_len, fc=fc),
        grid=(t_rows // tm,),
        in_specs=[
            pl.BlockSpec((tm, d), lambda i: (i, 0)),
            prev, nxt,
            pl.BlockSpec((None, 6, d), mod_map),
            pl.BlockSpec((1, d), lambda i: (0, 0)),
            resident(w_up.shape),
            pl.BlockSpec((3, fdim), lambda i: (0, 0)),
            pl.BlockSpec((1, fdim), lambda i: (0, 0)),
            resident(w_down.shape),
        ],
        out_specs=pl.BlockSpec((tm, d), lambda i: (i, 0)),
        out_shape=jax.ShapeDtypeStruct((t_rows, d), F32),
        scratch_shapes=[pltpu.VMEM((tm + 2 * HALO, d), BF16),
                        pltpu.VMEM((tm, fdim), BF16)],
        compiler_params=_params(("parallel",)),
        name=name,
    )(x, x, x, mod, g.reshape(1, d), w_up, conv_w, conv_b.reshape(1, fdim), w_down)


def _retention_kernel(df_ref, db_ref, q_ref, k_ref, v_ref, kc_ref, vc_ref, o_ref, kv_ref, ss_ref):
    n = q_ref.shape[0]
    lc = kc_ref.shape[0]
    c = RET_CHUNK
    nc = n // c
    pos = lax.broadcasted_iota(jnp.int32, (c, 1), 0).astype(F32)
    ii = lax.broadcasted_iota(jnp.int32, (c, c), 0)
    jj = lax.broadcasted_iota(jnp.int32, (c, c), 1)
    dij = (ii - jj).astype(F32)
    jpos = lax.broadcasted_iota(jnp.int32, (lc, 1), 0).astype(F32)
    lo = _half_mask((1, LANES))
    row_lo = lax.broadcasted_iota(jnp.int32, (LANES, 1), 0) < (LANES // 2)
    block_diag = (lax.broadcasted_iota(jnp.int32, (LANES, 2 * HEAD_V), 0) // (LANES // 2)
                  == lax.broadcasted_iota(jnp.int32, (LANES, 2 * HEAD_V), 1) // HEAD_V)
    zeros_v = jnp.zeros((c, HEAD_V), BF16)

    def cat(a, b):
        return jnp.concatenate([a, b], axis=-1)

    for pair in range(HEADS // 2):
        slab = slice(pair * LANES, (pair + 1) * LANES)
        vcol = slice(pair * 2 * HEAD_V, (pair + 1) * 2 * HEAD_V)
        lg = []
        for d_ref in (df_ref, db_ref):
            lg.append([-jnp.exp(jnp.full((1, 1), d_ref[2 * pair + e], F32)) for e in range(2)])
        (lgf0, lgf1), (lgb0, lgb1) = lg
        lgf_lane = jnp.where(lo, lgf0, lgf1)
        lgb_lane = jnp.where(lo, lgb0, lgb1)
        lgf_row = jnp.where(row_lo, lgf0, lgf1)
        lgb_row = jnp.where(row_lo, lgb0, lgb1)
        qdec = cat(jnp.exp(lgf_lane * (pos + 1.0)), jnp.exp(lgb_lane * (c - pos)))
        kdec = cat(jnp.exp(lgf_lane * (c - 1.0 - pos)), jnp.exp(lgb_lane * pos))
        cdec_f = jnp.exp(lgf_row * float(c))
        cdec_b = jnp.exp(lgb_row * float(c))
        intra = [jnp.where(dij >= 0, jnp.exp(f * jnp.maximum(dij, 0.0)), 0.0)
                 + jnp.where(dij <= 0, jnp.exp(b * jnp.maximum(-dij, 0.0)), 0.0)
                 for f, b in ((lgf0, lgb0), (lgf1, lgb1))]

        for ci in range(nc):
            rows = slice(ci * c, (ci + 1) * c)
            k = k_ref[rows, slab].astype(F32)
            kv_ref[ci] = _dot_tn((cat(k, k) * kdec).astype(BF16), v_ref[rows, vcol].astype(BF16))
        kc = kc_ref[:, slab].astype(F32)
        cdec0 = cat(jnp.exp(lgf_lane * (lc - 1.0 - jpos)), jnp.exp(lgb_lane * jpos))
        s0 = _dot_tn((cat(kc, kc) * cdec0).astype(BF16), vc_ref[:, vcol].astype(BF16))

        s = s0[0:LANES]
        for ci in range(nc):
            ss_ref[ci, 0:LANES, :] = jnp.where(block_diag, s, 0.0).astype(BF16)
            s = s * cdec_f + kv_ref[ci, 0:LANES, :]
        s = s0[LANES:2 * LANES]
        for ci in reversed(range(nc)):
            ss_ref[ci, LANES:2 * LANES, :] = jnp.where(block_diag, s, 0.0).astype(BF16)
            s = s * cdec_b + kv_ref[ci, LANES:2 * LANES, :]

        for ci in range(nc):
            rows = slice(ci * c, (ci + 1) * c)
            q = q_ref[rows, slab].astype(F32)
            kb = k_ref[rows, slab].astype(BF16)
            v = v_ref[rows, vcol].astype(BF16)
            att0 = _dot_nt(jnp.where(lo, q, 0.0).astype(BF16), kb) * intra[0]
            att1 = _dot_nt(jnp.where(lo, 0.0, q).astype(BF16), kb) * intra[1]
            v_blocks = jnp.concatenate([cat(v[:, 0:HEAD_V], zeros_v), cat(zeros_v, v[:, HEAD_V:])], axis=0)
            o_ref[rows, vcol] = (_dot(cat(att0, att1).astype(BF16), v_blocks)
                                 + _dot((cat(q, q) * qdec).astype(BF16), ss_ref[ci])).astype(o_ref.dtype)


def _retention(dec_f, dec_b, q, k, v, kc, vc, seq_len, ctx_len, name):
    t_rows = q.shape[0]
    batch = t_rows // seq_len
    qw, vw = q.shape[1], v.shape[1]
    nc = seq_len // RET_CHUNK
    smem = pl.BlockSpec(memory_space=pltpu.SMEM)
    return pl.pallas_call(
        _retention_kernel,
        grid=(batch,),
        in_specs=[
            smem, smem,
            pl.BlockSpec((seq_len, qw), lambda b: (b, 0)),
            pl.BlockSpec((seq_len, qw), lambda b: (b, 0)),
            pl.BlockSpec((seq_len, vw), lambda b: (b, 0)),
            pl.BlockSpec((ctx_len, qw), lambda b: (b, 0)),
            pl.BlockSpec((ctx_len, vw), lambda b: (b, 0)),
        ],
        out_specs=pl.BlockSpec((seq_len, vw), lambda b: (b, 0)),
        out_shape=jax.ShapeDtypeStruct((t_rows, vw), BF16),
        scratch_shapes=[pltpu.VMEM((nc, 2 * LANES, 2 * HEAD_V), F32),
                        pltpu.VMEM((nc, 2 * LANES, 2 * HEAD_V), BF16)],
        compiler_params=_params(("parallel",)),
        name=name,
    )(dec_f, dec_b, q, k, v, kc, vc)


def _out_odd_kernel(x_ref, pv_ref, pp_ref, pn_ref, gt_ref, o_ref, mod_ref, w_ref, pw_ref, ps_ref, gg_ref, y_ref, *,
                    seq_len):
    tm = x_ref.shape[0]
    r0 = lax.rem(pl.program_id(0) * tm, seq_len)
    pos = r0 + lax.broadcasted_iota(jnp.int32, (tm, 1), 0)
    has_prev = (r0 != 0).astype(F32)
    has_next = (r0 + tm != seq_len).astype(F32)
    rows = tm + 2 * HALO
    parts = []
    for gi, r in enumerate(POOL_RADII):
        col = slice(gi * LANES, (gi + 1) * LANES)
        v = pv_ref[:, col].astype(F32)
        ext = jnp.concatenate([pp_ref[:, col].astype(F32) * has_prev, v, pn_ref[:, col].astype(F32) * has_next],
                              axis=0)
        win = ext
        span = 1
        while span < 2 * r:
            win = win + pltpu.roll(win, rows - span, axis=0)
            span *= 2
        total = pltpu.roll(win, r, axis=0) + pltpu.roll(ext, rows - r, axis=0)
        total = total[HALO:HALO + tm]
        cnt = (jnp.minimum(pos + r + 1, seq_len) - jnp.maximum(pos - r, 0)).astype(F32)
        y = (total / cnt - v).astype(BF16)
        parts.append(_dot(y, pw_ref[gi]))
    y_p = jnp.concatenate(parts, axis=-1) * ps_ref[...]
    gate = gt_ref[...].astype(F32)
    y_r =jnp.concatenate(_head_rms(o_ref[...], gg_ref[...]), axis=-1) * (gate * _sigmoid(gate))
    z = jnp.concatenate([y_p.astype(BF16), y_r.astype(BF16)], axis=-1)
    y_ref[...] = x_ref[...] + mod_ref[2:3, :] * _dot(z, w_ref[...])


def _out_odd(x, pv, gate, o, mod, w_out, pool_w, pool_scale, gn_g, seq_len, tm, name):
    t_rows, d = x.shape
    pw = pv.shape[1]
    assert seq_len % tm == 0
    tiles_per_seq = seq_len // tm
    prev, nxt = _halo_specs(tm, HALO, pw, t_rows)
    return pl.pallas_call(
        functools.partial(_out_odd_kernel, seq_len=seq_len),
        grid=(t_rows // tm,),
        in_specs=[
            pl.BlockSpec((tm, d), lambda i: (i, 0)),
            pl.BlockSpec((tm, pw), lambda i: (i, 0)),
            prev, nxt,
            pl.BlockSpec((tm, gate.shape[1]), lambda i: (i, 0)),
            pl.BlockSpec((tm, o.shape[1]), lambda i: (i, 0)),
            pl.BlockSpec((None, 6, d), lambda i: (i // tiles_per_seq, 0, 0)),
            pl.BlockSpec(w_out.shape, lambda i: (0, 0)),
            pl.BlockSpec(pool_w.shape, lambda i: (0, 0, 0)),
            pl.BlockSpec((1, pw), lambda i: (0, 0)),
            pl.BlockSpec((1, HEAD_V), lambda i: (0, 0)),
        ],
        out_specs=pl.BlockSpec((tm, d), lambda i: (i, 0)),
        out_shape=jax.ShapeDtypeStruct((t_rows, d), F32),
        compiler_params=_params(("parallel",)),
        name=name,
    )(x, pv, pv, pv, gate, o, mod, w_out, pool_w, pool_scale.reshape(1, pw), gn_g.reshape(1, HEAD_V))


def _rope_tables(seq_len):
    rows = seq_len // GRID_W
    row = jnp.repeat(jnp.arange(rows, dtype=F32), GRID_W)
    col = jnp.tile(jnp.arange(GRID_W, dtype=F32), rows)
    inv = ROPE_BASE ** (-jnp.arange(ROPE_FREQS, dtype=F32) / ROPE_FREQS)
    ar, ac = row[:, None] * inv, col[:, None] * inv
    cos = jnp.concatenate([jnp.cos(ar), jnp.cos(ar), jnp.cos(ac), jnp.cos(ac)], axis=-1)
    sin = jnp.concatenate([-jnp.sin(ar), jnp.sin(ar), -jnp.sin(ac), jnp.sin(ac)], axis=-1)
    reps = LANES // ROPE_HEAD_DIM
    return jnp.tile(cos, (1, reps)), jnp.tile(sin, (1, reps))


def _tile(total, want):
    t = min(want, total)
    assert total % t == 0
    return t


def kernel(x, c, ctx, c_ctx, mod_w, mod_b, norm1_g, norm2_g, ev_w_in, ev_w_out, sc_conv_w, da_q_norm, da_k_norm,
           da_lq1, da_lk1, da_lq2, da_lk2, da_subln_g, od_w_in, od_w_out, pool_w, pool_scale, ret_decay_f,
           ret_decay_b, ret_gn_g, ffn_w_up, ffn_conv_w, ffn_conv_b, ffn_w_down):
    batch, seq, d = x.shape
    ctx_len = ctx.shape[1]
    depth = mod_w.shape[0]
    t_lat, t_ctx = batch * seq, batch * ctx_len
    xs = x.reshape(t_lat, d)
    cs = ctx.reshape(t_ctx, d)

    pad = (-(batch + 1)) % 8
    c_all = jnp.concatenate([c, c_ctx[None, :], jnp.zeros((pad, d), F32)], axis=0)
    mods = _modulation(c_all, mod_w, mod_b)
    rope = _rope_tables(seq)

    tm_lat = _tile(seq, 512)
    tm_ctx = _tile(t_ctx, 512)
    tm_ffn_lat = _tile(seq, 1024)
    tm_ffn_ctx = _tile(t_ctx, 1024)
    tq = _tile(seq, ATTN_SUB_ROWS)
    fdim = ffn_w_down.shape[1]
    fc = 256 if fdim % 256 == 0 else LANES

    for i in range(depth):
        last = i == depth - 1
        j = i // 2
        mod_l = mods[i, :batch].reshape(batch, 6, d)
        mod_c = mods[i, batch:batch + 1].reshape(1, 6, d)
        w_up = ffn_w_up[i].astype(BF16)
        w_down = ffn_w_down[i].astype(BF16)
        if i % 2 == 0:
            w_in = ev_w_in[j].astype(BF16)
            w_out = ev_w_out[j].astype(BF16)
            scw = ev_w_out.shape[1] // 2
            qkw = (w_in.shape[1] - 4 * scw) // 2
            q_scale = ROPE_HEAD_DIM ** -0.5 * LOG2E
            segs = [(3 * scw, 3 * scw + qkw, "qknorm", 0, q_scale),
                    (3 * scw + qkw, 3 * scw + 2 * qkw, "qknorm", 1, 1.0),
                    (0, 3 * scw, "copy", 0, 1.0),
                    (3 * scw + 2 * qkw, w_in.shape[1], "copy", 0, 1.0)]
            dts = [BF16, BF16, BF16, BF16]
            gains = [da_q_norm[j], da_k_norm[j]]
            lam_init = 0.8 - 0.6 * math.exp(-0.3 * i)
            lam_vecs = jnp.stack([da_lq1[j], da_lk1[j], da_lq2[j], da_lk2[j]], axis=0)
            bound = (ROPE_HEAD_DIM * q_scale * BOUND_MARGIN * jnp.max(jnp.abs(da_q_norm[j]))
                     * jnp.max(jnp.abs(da_k_norm[j]))).reshape(1).astype(F32)
            q_l, k_l, g_l, v_l = _in_proj(xs, mod_l, norm1_g[i], w_in, segs, dts, gains, rope, seq, tm_lat,
                                          "in_even_lat")
            q_c, k_c, g_c, v_c = _in_proj(cs, mod_c, norm1_g[i], w_in, segs, dts, gains, None, ctx_len, tm_ctx,
                                          "in_even_ctx")
            o_l = _diff_attn(bound, lam_vecs,q_l, [(k_l, v_l, seq), (k_c, v_c, ctx_len)], seq, tq, lam_init, "attn_lat")
            xs_new = _out_even(xs, g_l, o_l, mod_l, w_out, sc_conv_w[j], da_subln_g[j], seq, tm_lat,
                               1.0 - lam_init, "out_even_lat")
            if not last:
                o_c = _diff_attn(bound, lam_vecs,q_c, [(k_c, v_c, ctx_len)], ctx_len, ctx_len, lam_init, "attn_ctx")
                cs = _out_even(cs, g_c, o_c, mod_c, w_out, sc_conv_w[j], da_subln_g[j], ctx_len, tm_ctx,
                               1.0 - lam_init, "out_even_ctx")
            xs = xs_new
        else:
            assert last, "an odd layer that still updates the context stream is not implemented"
            w_in = od_w_in[j].astype(BF16)
            w_out = od_w_out[j].astype(BF16)
            pw = pool_scale.shape[1]
            vw = HEADS * HEAD_V
            qw = (w_in.shape[1] - pw - 2 * vw) // 2
            segs = [(pw, pw + qw, "rope", 0, ROPE_HEAD_DIM ** -0.5),
                    (pw + qw, pw + 2 * qw, "rope", 0, 1.0),
                    (0, pw, "copy", 0, 1.0),
                    (pw + 2 * qw, pw + 2 * qw + vw, "copy", 0, 1.0),
                    (pw + 2 * qw + vw, w_in.shape[1], "copy", 0, 1.0)]
            q_l, k_l, pv, v_l, gate = _in_proj(xs, mod_l, norm1_g[i], w_in, segs, [BF16] * 5, [], rope, seq, tm_lat,
                                               "in_odd_lat")
            w_kv = w_in[:, pw + qw:pw + 2 * qw + vw]
            k_c, v_c = _in_proj(cs, mod_c, norm1_g[i], w_kv, [(0, qw, "copy", 0, 1.0), (qw, qw + vw, "copy", 0, 1.0)],
                                [BF16, BF16], [], None, ctx_len, tm_ctx, "in_odd_ctx")
            o_l = _retention(ret_decay_f[j], ret_decay_b[j], q_l, k_l, v_l, k_c, v_c, seq, ctx_len, "retention")
            xs = _out_odd(xs, pv, gate, o_l, mod_l, w_out, pool_w[j].astype(BF16), pool_scale[j], ret_gn_g[j],
                          seq, tm_lat, "out_odd_lat")
        xs = _ffn(xs, mod_l, norm2_g[i], w_up, ffn_conv_w[i], ffn_conv_b[i], w_down, seq, tm_ffn_lat, fc, "ffn_lat")
        if not last:
            cs = _ffn(cs, mod_c, norm2_g[i], w_up, ffn_conv_w[i], ffn_conv_b[i], w_down, ctx_len, tm_ffn_ctx, fc,
                      "ffn_ctx")
    return xs.reshape(batch, seq, d)
```

```python
import functools
import math

import jax
import jax.numpy as jnp
from jax import lax
from jax.experimental import pallas as pl
from jax.experimental.pallas import tpu as pltpu

F32 = jnp.float32
BF16 = jnp.bfloat16

EPS = 1e-6
GRID_W = 64
ROPE_HEAD_DIM = 64
ROPE_FREQS = ROPE_HEAD_DIM // 4
ROPE_BASE = 10000.0
HEADS = 4
HEAD_V = 128
RET_CHUNK = 128
POOL_RADII = (1, 2, 4, 8)
LANES = 128
HALO = 16
VMEM_LIMIT = 56 * 1024 * 1024
LOG2E = math.log2(math.e)
BOUND_MARGIN = 1.02
MAX_SAFE_SCORE_BOUND = 40.0
ATTN_SUB_ROWS = 1024


def _params(sem, vmem=VMEM_LIMIT):
    return pltpu.CompilerParams(dimension_semantics=sem, vmem_limit_bytes=vmem)


def _sigmoid(x):
    return 1.0 / (1.0 + jnp.exp(-x))


def _dot(a, b):
    return jnp.dot(a, b, preferred_element_type=F32)


def _dot_nt(a, b):
    return lax.dot_general(a, b, (((1,), (1,)), ((), ())), preferred_element_type=F32)


def _dot_tn(a, b):
    return lax.dot_general(a, b, (((0,), (0,)), ((), ())), preferred_element_type=F32)


def _rms(x, width):
    return x * lax.rsqrt(jnp.sum(x * x, axis=-1, keepdims=True) * (1.0 / width) + EPS)


def _norm_mod(x, g, shift, scale):
    return (_rms(x, x.shape[-1]) * g) * (1.0 + scale) + shift


def _half_mask(shape):
    return lax.broadcasted_iota(jnp.int32, shape, len(shape) - 1) < (LANES // 2)


def _rope(t, cos, sin):
    lane = lax.broadcasted_iota(jnp.int32, t.shape, 1)
    first = lax.rem(lane, 2 * ROPE_FREQS) < ROPE_FREQS
    partner = jnp.where(first, pltpu.roll(t, LANES - ROPE_FREQS, axis=1), pltpu.roll(t, ROPE_FREQS, axis=1))
    return t * cos + partner * sin


def _norm64(t, gain):
    lo = _half_mask(t.shape)
    sq = t * t
    s_lo = jnp.sum(jnp.where(lo, sq, 0.0), axis=-1, keepdims=True)
    s_hi = jnp.sum(jnp.where(lo, 0.0, sq), axis=-1, keepdims=True)
    w = 1.0 / (LANES // 2)
    r = jnp.where(lo, lax.rsqrt(s_lo * w + EPS), lax.rsqrt(s_hi * w + EPS))
    return (t * r) * gain


def _mod_kernel(c_ref, w_ref, b_ref, o_ref):
    c = c_ref[...]
    a = c * _sigmoid(c)
    o_ref[...] = _dot(a.astype(BF16), w_ref[...].astype(BF16)) + b_ref[...]


def _modulation(c_all, mod_w, mod_b):
    depth, d, n6 = mod_w.shape
    rows = c_all.shape[0]
    tn = 1536
    return pl.pallas_call(
        _mod_kernel,
        grid=(depth, n6 // tn),
        in_specs=[
            pl.BlockSpec((rows, d), lambda l, j: (0, 0)),
            pl.BlockSpec((None, d, tn), lambda l, j: (l, 0, j)),
            pl.BlockSpec((None, 1, tn), lambda l, j: (l, 0, j)),
        ],
        out_specs=pl.BlockSpec((None, rows, tn), lambda l, j: (l, 0, j)),
        out_shape=jax.ShapeDtypeStruct((depth, rows, n6), F32),
        compiler_params=_params(("parallel", "parallel")),
        name="modulation",
    )(c_all, mod_w, mod_b.reshape(depth, 1, n6))


def _in_proj_kernel(*refs, segs, rotate, n_gain):
    x_ref, mod_ref, g_ref, w_ref = refs[:4]
    gain_refs = refs[4:4 + n_gain]
    pos = 4 + n_gain
    if rotate:
        cos_ref, sin_ref = refs[pos:pos + 2]
        pos += 2
    out_refs = refs[pos:]
    h = _norm_mod(x_ref[...], g_ref[...], mod_ref[0:1, :], mod_ref[1:2, :]).astype(BF16)
    for (c0, c1, mode, gain_idx, scale), o_ref in zip(segs, out_refs):
        y = _dot(h, w_ref[:, c0:c1])
        if mode == "copy":
            o_ref[...] = y.astype(o_ref.dtype)
            continue
        for s in range((c1 - c0) // LANES):
            t = y[:, s * LANES:(s + 1) * LANES]
            if mode == "qknorm":
                t = _norm64(t, gain_refs[gain_idx][...])
            if rotate:
                t = _rope(t, cos_ref[...], sin_ref[...])
            if scale != 1.0:
                t = t * scale
            o_ref[:, s * LANES:(s + 1) * LANES] = t.astype(o_ref.dtype)


def _in_proj(x, mod, g, w, segs, out_dtypes, gains, rope, seq_len, tm, name):
    t_rows, d = x.shape
    per_seq = mod.shape[0] > 1
    tiles_per_seq = max(seq_len // tm, 1)
    if per_seq:
        assert seq_len % tm == 0
        mod_map = lambda i: (i // tiles_per_seq, 0, 0)
    else:
        mod_map = lambda i: (0, 0, 0)
    in_specs = [
        pl.BlockSpec((tm, d), lambda i: (i, 0)),
        pl.BlockSpec((None, 6, d), mod_map),
        pl.BlockSpec((1, d), lambda i: (0, 0)),
        pl.BlockSpec(w.shape, lambda i: (0, 0)),
    ]
    args = [x, mod, g.reshape(1, d), w]
    for gn in gains:
        in_specs.append(pl.BlockSpec((1, LANES), lambda i: (0, 0)))
        args.append(jnp.tile(gn.reshape(1, -1), (1, LANES // gn.shape[-1])))
    if rope is not None:
        assert seq_len % tm == 0
        for tab in rope:
            in_specs.append(pl.BlockSpec((tm, LANES), lambda i: (i % tiles_per_seq, 0)))
            args.append(tab)
    out_specs = [pl.BlockSpec((tm, c1 - c0), lambda i: (i, 0)) for (c0, c1, _, _, _) in segs]
    out_shape = [jax.ShapeDtypeStruct((t_rows, c1 - c0), dt) for (c0, c1, _, _, _), dt in zip(segs, out_dtypes)]
    return pl.pallas_call(
        functools.partial(_in_proj_kernel, segs=tuple(segs), rotate=rope is not None, n_gain=len(gains)),
        grid=(t_rows // tm,),
        in_specs=in_specs,
        out_specs=out_specs,
        out_shape=out_shape,
        compiler_params=_params(("parallel",)),
        name=name,
    )(*args)


def _diff_attn_kernel(*refs, n_src, lam_init, sub_rows):
    bound_ref, lam_ref, q_ref = refs[:3]
    kv_refs = refs[3:3 + 2 * n_src]
    o_ref = refs[3 + 2 * n_src]
    lv = lam_ref[...]
    lam = (jnp.exp(jnp.sum(lv[0:1] * lv[1:2], axis=-1, keepdims=True))
           - jnp.exp(jnp.sum(lv[2:3] * lv[3:4], axis=-1, keepdims=True)) + lam_init)
    bound = bound_ref[0]

    def attend(use_bound):
        k0, k1, vs = [], [], []
        for j in range(n_src):
            k = kv_refs[2 * j][...]
            lo = _half_mask(k.shape)
            zero = jnp.zeros_like(k)
            k0.append(jnp.where(lo, k, zero))
            k1.append(jnp.where(lo, zero, k))
            vs.append(kv_refs[2 * j + 1][...])
        for r in range(q_ref.shape[0] // sub_rows):
            attend_rows(slice(r * sub_rows, (r + 1) * sub_rows), k0, k1, vs, use_bound)

    def attend_rows(rows, k0, k1, vs, use_bound):
        q = q_ref[rows, :]

        def softmax_av(ks):
            ss = [_dot_nt(k, q) for k in ks]
            if use_bound:
                m = bound
            else:
                m = functools.reduce(jnp.maximum, [jnp.max(s, axis=0, keepdims=True) for s in ss])
            ps = [jnp.exp2(s - m) for s in ss]
            l = functools.reduce(jnp.add, [jnp.sum(p, axis=0, keepdims=True) for p in ps])
            acc = functools.reduce(jnp.add, [_dot_tn(v, p.astype(BF16)) for v, p in zip(vs, ps)])
            return acc, l

        acc0, l0 = softmax_av(k0)
        acc1, l1 = softmax_av(k1)
        o_t = acc0 * (1.0 / l0) - acc1 * (lam / l1)
        o_ref[rows, :] = o_t.T.astype(o_ref.dtype)

    safe = bound <= MAX_SAFE_SCORE_BOUND

    @pl.when(safe)
    def _():
        attend(True)

    @pl.when(jnp.logical_not(safe))
    def _():
        attend(False)


def _diff_attn(bound, lam_vecs, q, kvs, seq_q, tq, lam_init, name):
    t_rows, width = q.shape
    batch = t_rows // seq_q
    nq = seq_q // tq
    in_specs = [
        pl.BlockSpec(memory_space=pltpu.SMEM),
        pl.BlockSpec(lam_vecs.shape, lambda b, h, i: (0, 0)),
        pl.BlockSpec((tq, HEAD_V), lambda b, h, i: (b * nq + i, h)),
    ]
    args = [bound, lam_vecs, q]
    for k, v, seq_k in kvs:
        in_specs.append(pl.BlockSpec((seq_k, HEAD_V), lambda b, h, i: (b, h)))
        in_specs.append(pl.BlockSpec((seq_k, HEAD_V), lambda b, h, i: (b, h)))
        args += [k, v]
    return pl.pallas_call(
        functools.partial(_diff_attn_kernel, n_src=len(kvs), lam_init=lam_init, sub_rows=min(tq, ATTN_SUB_ROWS)),
        grid=(batch, width // HEAD_V, nq),
        in_specs=in_specs,
        out_specs=pl.BlockSpec((tq, HEAD_V), lambda b, h, i: (b * nq + i, h)),
        out_shape=jax.ShapeDtypeStruct((t_rows, width), BF16),
        compiler_params=_params(("parallel", "parallel", "parallel")),
        name=name,
    )(*args)


def _halo_specs(tm, halo, width, t_rows, col=0):
    per = tm // halo
    last = t_rows // halo - 1
    prev = pl.BlockSpec((halo, width), lambda i, *_: (jnp.maximum(i * per - 1, 0), col))
    nxt = pl.BlockSpec((halo, width), lambda i, *_: (jnp.minimum((i + 1) * per, last), col))
    return prev, nxt


def _three(arr, tm, col=0):
    t_rows, width = arr.shape
    prev, nxt = _halo_specs(tm, HALO, width, t_rows, col)
    return [(arr, prev), (arr, pl.BlockSpec((tm, width), lambda i: (i, col))), (arr, nxt)]


def _cat_rows(refs):
    return jnp.concatenate([r[...] for r in refs], axis=0)


def _tile_rows(tm, seq_len):
    row = lax.broadcasted_iota(jnp.int32, (tm + 2 * HALO, 1), 0)
    if tm > seq_len:
        return lax.rem(pl.program_id(0) * tm + (seq_len - HALO) + row, seq_len), None
    pos = lax.rem(pl.program_id(0) * tm, seq_len) - HALO + row
    return pos, jnp.logical_and(pos >= 0, pos < seq_len).astype(F32)


def _shift_rows(u, pos, own, seq_len):
    rows = u.shape[0]
    dn = pltpu.roll(u, 1, axis=0)
    up = pltpu.roll(u, rows - 1, axis=0)
    if own is None:
        dn = jnp.where(pos == 0, 0.0, dn)
        up = jnp.where(pos == seq_len - 1, 0.0, up)
    return dn, up


def _head_rms(o, gain, post_scale=1.0):
    parts = []
    for h in range(o.shape[-1] // HEAD_V):
        t = _rms(o[:, h * HEAD_V:(h + 1) * HEAD_V].astype(F32), HEAD_V) * gain
        parts.append(t * post_scale if post_scale != 1.0 else t)
    return parts


def _even_mix(gate_refs, o_refs, cw_ref, sg_ref, pos, own, seq_len, post_scale):
    gates = _cat_rows(gate_refs).astype(F32)
    w = gates.shape[1] // 3
    u = gates[:, w:2 * w] * gates[:, 2 * w:3 * w]
    if own is not None:
        u = u * own
    dn, up = _shift_rows(u, pos, own, seq_len)
    y_conv = gates[:, 0:w] * (dn * cw_ref[0:1, :] + u * cw_ref[1:2, :] + up * cw_ref[2:3, :])
    heads = _head_rms(_cat_rows(o_refs), sg_ref[...], post_scale)
    return jnp.concatenate([y_conv.astype(BF16)] + [p.astype(BF16) for p in heads], axis=-1)


def _mixer_ffn_kernel(*refs, kind, n_mix, seq_len, fc, post_scale):
    x_refs = refs[0:3]
    mix_refs = [refs[3 + 3 * k:6 + 3 * k] for k in range(n_mix)]
    p = 3 + 3 * n_mix
    mod_ref, n2_ref, wo_ref = refs[p:p + 3]
    n_par = 2 if kind == "even" else 3
    mix_par = refs[p + 3:p + 3 + n_par]
    wu_ref, cw_ref, cb_ref, wd_ref, y_ref, h_ref, act_ref = refs[p + 3 + n_par:]
    tm = y_ref.shape[0]
    rows = tm + 2 * HALO
    fdim = wd_ref.shape[0]
    pos, own = _tile_rows(tm, seq_len)

    if kind == "even":
        z = _even_mix(mix_refs[0], mix_refs[1], mix_par[0], mix_par[1], pos, own, seq_len, post_scale)
    else:
        z = _odd_mix(mix_refs[0], mix_refs[1], mix_refs[2], mix_par[0], mix_par[1], mix_par[2], pos, own, seq_len)
    x1 = _cat_rows(x_refs) + mod_ref[2:3, :] * _dot(z, wo_ref[...])
    y_ref[...] = x1[HALO:HALO + tm]
    h = _norm_mod(x1, n2_ref[...], mod_ref[3:4, :], mod_ref[4:5, :])
    if own is not None:
        h = h * own
    h_ref[...] = h.astype(BF16)

    for j in range(fdim // fc):
        cols = slice(j * fc, (j + 1) * fc)
        a_ext = _dot(h_ref[...], wu_ref[:, cols])
        dn = pltpu.roll(a_ext, 1, axis=0)[HALO:HALO + tm]
        up = pltpu.roll(a_ext, rows - 1, axis=0)[HALO:HALO + tm]
        if own is None:
            dn = jnp.where(pos[HALO:HALO + tm] == 0, 0.0, dn)
            up = jnp.where(pos[HALO:HALO + tm] == seq_len - 1, 0.0, up)
        a = (dn * cw_ref[0:1, cols] + a_ext[HALO:HALO + tm] * cw_ref[1:2, cols] + up * cw_ref[2:3, cols]
             + cb_ref[:, cols])
        b = _dot(h_ref[HALO:HALO + tm, :], wu_ref[:, fdim + j * fc:fdim + (j + 1) * fc])
        act_ref[:, cols] = (a * _sigmoid(a) * b).astype(BF16)
    y_ref[...] += mod_ref[5:6, :] * _dot(act_ref[...], wd_ref[...])


def _mixer_ffn(kind, x, mix_arrays, mod, w_out, mix_params, norm_g, w_up, conv_w, conv_b, w_down, seq_len, tm, fc,
               post_scale, name):
    t_rows, d = x.shape
    fdim = w_down.shape[0]
    per_seq = mod.shape[0] > 1
    assert tm % HALO == 0 and (seq_len % tm == 0 or tm % seq_len == 0)
    assert seq_len % tm == 0 or not per_seq
    tiles_per_seq = max(seq_len // tm, 1)
    mod_map = (lambda i: (i // tiles_per_seq, 0, 0)) if per_seq else (lambda i: (0, 0, 0))

    def whole(arr, single=False):
        mode = dict(pipeline_mode=pl.Buffered(1)) if single else {}
        return arr, pl.BlockSpec(arr.shape, lambda i: (0,) * arr.ndim, **mode)

    operands = _three(x, tm)
    for arr in mix_arrays:
        operands += _three(arr, tm)
    operands += [(mod, pl.BlockSpec((None, 6, d), mod_map)), whole(norm_g.reshape(1, d)), whole(w_out, True)]
    operands += [whole(a) for a in mix_params]
    operands += [whole(w_up, True), whole(conv_w), whole(conv_b.reshape(1, fdim)), whole(w_down, True)]
    return pl.pallas_call(
        functools.partial(_mixer_ffn_kernel, kind=kind, n_mix=len(mix_arrays), seq_len=seq_len, fc=fc,
                          post_scale=post_scale),
        grid=(t_rows // tm,),
        in_specs=[spec for _, spec in operands],
        out_specs=pl.BlockSpec((tm, d), lambda i: (i, 0)),
        out_shape=jax.ShapeDtypeStruct((t_rows, d), F32),
        scratch_shapes=[pltpu.VMEM((tm + 2 * HALO, d), BF16),
                        pltpu.VMEM((tm, fdim), BF16)],
        compiler_params=_params(("parallel",)),
        name=name,
    )(*[arr for arr, _ in operands])


def _retention_kernel(df_ref, db_ref, q_ref, k_ref, v_ref, kc_ref, vc_ref, o_ref, kv_ref, ss_ref):
    n = q_ref.shape[0]
    lc = kc_ref.shape[0]
    c = RET_CHUNK
    nc = n // c
    pos = lax.broadcasted_iota(jnp.int32, (c, 1), 0).astype(F32)
    ii = lax.broadcasted_iota(jnp.int32, (c, c), 0)
    jj = lax.broadcasted_iota(jnp.int32, (c, c), 1)
    dij = (ii - jj).astype(F32)
    jpos = lax.broadcasted_iota(jnp.int32, (lc, 1), 0).astype(F32)
    lo = _half_mask((1, LANES))
    row_lo = lax.broadcasted_iota(jnp.int32, (LANES, 1), 0) < (LANES // 2)
    block_diag = (lax.broadcasted_iota(jnp.int32, (LANES, 2 * HEAD_V), 0) // (LANES // 2)
                  == lax.broadcasted_iota(jnp.int32, (LANES, 2 * HEAD_V), 1) // HEAD_V)
    zeros_v = jnp.zeros((c, HEAD_V), BF16)

    def cat(a, b):
        return jnp.concatenate([a, b], axis=-1)

    for pair in range(HEADS // 2):
        slab = slice(pair * LANES, (pair + 1) * LANES)
        vcol = slice(pair * 2 * HEAD_V, (pair + 1) * 2 * HEAD_V)
        lg = []
        for d_ref in (df_ref, db_ref):
            lg.append([-jnp.exp(jnp.full((1, 1), d_ref[2 * pair + e], F32)) for e in range(2)])
        (lgf0, lgf1), (lgb0, lgb1) = lg
        lgf_lane = jnp.where(lo, lgf0, lgf1)
        lgb_lane = jnp.where(lo, lgb0, lgb1)
        lgf_row = jnp.where(row_lo, lgf0, lgf1)
        lgb_row = jnp.where(row_lo, lgb0, lgb1)
        qdec = cat(jnp.exp(lgf_lane * (pos + 1.0)), jnp.exp(lgb_lane * (c - pos)))
        kdec = cat(jnp.exp(lgf_lane * (c - 1.0 - pos)), jnp.exp(lgb_lane * pos))
        cdec_f = jnp.exp(lgf_row * float(c))
        cdec_b = jnp.exp(lgb_row * float(c))
        intra = [jnp.where(dij >= 0, jnp.exp(f * jnp.maximum(dij, 0.0)), 0.0)
                 + jnp.where(dij <= 0, jnp.exp(b * jnp.maximum(-dij, 0.0)), 0.0)
                 for f, b in ((lgf0, lgb0), (lgf1, lgb1))]

        for ci in range(nc):
            rows = slice(ci * c, (ci + 1) * c)
            k = k_ref[rows, slab].astype(F32)
            kv_ref[ci] = _dot_tn((cat(k, k) * kdec).astype(BF16), v_ref[rows, vcol].astype(BF16))
        kc = kc_ref[:, slab].astype(F32)
        cdec0 = cat(jnp.exp(lgf_lane * (lc - 1.0 - jpos)), jnp.exp(lgb_lane * jpos))
        s0 = _dot_tn((cat(kc, kc) * cdec0).astype(BF16), vc_ref[:, vcol].astype(BF16))

        s = s0[0:LANES]
        for ci in range(nc):
            ss_ref[ci, 0:LANES, :] = jnp.where(block_diag, s, 0.0).astype(BF16)
            s = s * cdec_f + kv_ref[ci, 0:LANES, :]
        s = s0[LANES:2 * LANES]
        for ci in reversed(range(nc)):
            ss_ref[ci, LANES:2 * LANES, :] = jnp.where(block_diag, s, 0.0).astype(BF16)
            s = s * cdec_b + kv_ref[ci, LANES:2 * LANES, :]

        for ci in range(nc):
            rows = slice(ci * c, (ci + 1) * c)
            q = q_ref[rows, slab].astype(F32)
            kb = k_ref[rows, slab].astype(BF16)
            v = v_ref[rows, vcol].astype(BF16)
            att0 = _dot_nt(jnp.where(lo, q, 0.0).astype(BF16), kb) * intra[0]
            att1 = _dot_nt(jnp.where(lo, 0.0, q).astype(BF16), kb) * intra[1]
            v_blocks = jnp.concatenate([cat(v[:, 0:HEAD_V], zeros_v), cat(zeros_v, v[:, HEAD_V:])], axis=0)
            o_ref[rows, vcol] = (_dot(cat(att0, att1).astype(BF16), v_blocks)
                                 + _dot((cat(q, q) * qdec).astype(BF16), ss_ref[ci])).astype(o_ref.dtype)


def _retention(dec_f, dec_b, q, k, v, kc, vc, seq_len, ctx_len, name):
    t_rows = q.shape[0]
    batch = t_rows // seq_len
    qw, vw = q.shape[1], v.shape[1]
    nc = seq_len // RET_CHUNK
    smem = pl.BlockSpec(memory_space=pltpu.SMEM)
    return pl.pallas_call(
        _retention_kernel,
        grid=(batch,),
        in_specs=[
            smem, smem,
            pl.BlockSpec((seq_len, qw), lambda b: (b, 0)),
            pl.BlockSpec((seq_len, qw), lambda b: (b, 0)),
            pl.BlockSpec((seq_len, vw), lambda b: (b, 0)),
            pl.BlockSpec((ctx_len, qw), lambda b: (b, 0)),
            pl.BlockSpec((ctx_len, vw), lambda b: (b, 0)),
        ],
        out_specs=pl.BlockSpec((seq_len, vw), lambda b: (b, 0)),
        out_shape=jax.ShapeDtypeStruct((t_rows, vw), BF16),
        scratch_shapes=[pltpu.VMEM((nc, 2 * LANES, 2 * HEAD_V), F32),
                        pltpu.VMEM((nc, 2 * LANES, 2 * HEAD_V), BF16)],
        compiler_params=_params(("parallel",)),
        name=name,
    )(dec_f, dec_b, q, k, v, kc, vc)


def _odd_mix(pv_refs, gate_refs, o_refs, pw_ref, ps_ref, gg_ref, pos, own, seq_len):
    assert own is not None and max(POOL_RADII) < HALO
    pv = _cat_rows(pv_refs).astype(F32) * own
    rows = pv.shape[0]
    parts = []
    for gi, r in enumerate(POOL_RADII):
        v = pv[:, gi * LANES:(gi + 1) * LANES]
        win = v
        span = 1
        while span < 2 * r:
            win = win + pltpu.roll(win, rows - span, axis=0)
            span *= 2
        total = pltpu.roll(win, r, axis=0) + pltpu.roll(v, rows - r, axis=0)
        cnt = jnp.maximum(jnp.minimum(pos + r + 1, seq_len) - jnp.maximum(pos - r, 0), 1).astype(F32)
        y = (total / cnt - v).astype(BF16)
        parts.append(_dot(y, pw_ref[gi]))
    y_p = jnp.concatenate(parts, axis=-1) * ps_ref[...]
    gate = _cat_rows(gate_refs).astype(F32)
    y_r = jnp.concatenate(_head_rms(_cat_rows(o_refs), gg_ref[...]), axis=-1) * (gate * _sigmoid(gate))
    return jnp.concatenate([y_p.astype(BF16), y_r.astype(BF16)], axis=-1)


def _rope_tables(seq_len):
    rows = seq_len // GRID_W
    row = jnp.repeat(jnp.arange(rows, dtype=F32), GRID_W)
    col = jnp.tile(jnp.arange(GRID_W, dtype=F32), rows)
    inv = ROPE_BASE ** (-jnp.arange(ROPE_FREQS, dtype=F32) / ROPE_FREQS)
    ar, ac = row[:, None] * inv, col[:, None] * inv
    cos = jnp.concatenate([jnp.cos(ar), jnp.cos(ar), jnp.cos(ac), jnp.cos(ac)], axis=-1)
    sin = jnp.concatenate([-jnp.sin(ar), jnp.sin(ar), -jnp.sin(ac), jnp.sin(ac)], axis=-1)
    reps = LANES // ROPE_HEAD_DIM
    return jnp.tile(cos, (1, reps)), jnp.tile(sin, (1, reps))


def _tile(total, want):
    t = min(want, total)
    assert total % t == 0
    return t


def kernel(x, c, ctx, c_ctx, mod_w, mod_b, norm1_g, norm2_g, ev_w_in, ev_w_out, sc_conv_w, da_q_norm, da_k_norm,
           da_lq1, da_lk1, da_lq2, da_lk2, da_subln_g, od_w_in, od_w_out, pool_w, pool_scale, ret_decay_f,
           ret_decay_b, ret_gn_g, ffn_w_up, ffn_conv_w, ffn_conv_b, ffn_w_down):
    batch, seq, d = x.shape
    ctx_len = ctx.shape[1]
    depth = mod_w.shape[0]
    t_lat, t_ctx = batch * seq, batch * ctx_len
    xs = x.reshape(t_lat, d)
    cs = ctx.reshape(t_ctx, d)

    pad = (-(batch + 1)) % 8
    c_all = jnp.concatenate([c, c_ctx[None, :], jnp.zeros((pad, d), F32)], axis=0)
    mods = _modulation(c_all, mod_w, mod_b)
    rope = _rope_tables(seq)

    tm_lat = _tile(seq, 512)
    tm_ctx = _tile(t_ctx, 512)
    tm_ffn_lat = _tile(seq, 512)
    tm_ffn_ctx = _tile(t_ctx, 512)
    tq = _tile(seq, ATTN_SUB_ROWS)
    fdim = ffn_w_down.shape[1]
    fc = 256 if fdim % 256 == 0 else LANES

    for i in range(depth):
        last = i == depth - 1
        j = i // 2
        mod_l = mods[i, :batch].reshape(batch, 6, d)
        mod_c = mods[i, batch:batch + 1].reshape(1, 6, d)
        ffn_w = (ffn_w_up[i].astype(BF16), ffn_conv_w[i], ffn_conv_b[i], ffn_w_down[i].astype(BF16))
        if i % 2 == 0:
            w_in = ev_w_in[j].astype(BF16)
            w_out = ev_w_out[j].astype(BF16)
            scw = ev_w_out.shape[1] // 2
            qkw = (w_in.shape[1] - 4 * scw) // 2
            q_scale = ROPE_HEAD_DIM ** -0.5 * LOG2E
            segs = [(3 * scw, 3 * scw + qkw, "qknorm", 0, q_scale),
                    (3 * scw + qkw, 3 * scw + 2 * qkw, "qknorm", 1, 1.0),
                    (0, 3 * scw, "copy", 0, 1.0),
                    (3 * scw + 2 * qkw, w_in.shape[1], "copy", 0, 1.0)]
            dts = [BF16, BF16, BF16, BF16]
            gains = [da_q_norm[j], da_k_norm[j]]
            lam_init = 0.8 - 0.6 * math.exp(-0.3 * i)
            lam_vecs = jnp.stack([da_lq1[j], da_lk1[j], da_lq2[j], da_lk2[j]], axis=0)
            bound = (ROPE_HEAD_DIM * q_scale * BOUND_MARGIN * jnp.max(jnp.abs(da_q_norm[j]))
                     * jnp.max(jnp.abs(da_k_norm[j]))).reshape(1).astype(F32)
            q_l, k_l, g_l, v_l = _in_proj(xs, mod_l, norm1_g[i], w_in, segs, dts, gains, rope, seq, tm_lat,
                                          "in_even_lat")
            q_c, k_c, g_c, v_c = _in_proj(cs, mod_c, norm1_g[i], w_in, segs, dts, gains, None, ctx_len, tm_ctx,
                                          "in_even_ctx")
            o_l = _diff_attn(bound, lam_vecs,q_l, [(k_l, v_l, seq), (k_c, v_c, ctx_len)], seq, tq, lam_init, "attn_lat")
            mix_params = [sc_conv_w[j], da_subln_g[j].reshape(1, HEAD_V)]
            xs = _mixer_ffn("even", xs, [g_l, o_l], mod_l, w_out, mix_params, norm2_g[i], *ffn_w, seq, tm_ffn_lat, fc,
                            1.0 - lam_init, "mix_ffn_even_lat")
            if not last:
                o_c = _diff_attn(bound, lam_vecs, q_c, [(k_c, v_c, ctx_len)], ctx_len, ctx_len, lam_init, "attn_ctx")
                cs = _mixer_ffn("even", cs, [g_c, o_c], mod_c, w_out, mix_params, norm2_g[i], *ffn_w, ctx_len,
                                tm_ffn_ctx, fc, 1.0 - lam_init, "mix_ffn_even_ctx")
        else:
            assert last, "an odd layer that still updates the context stream is not implemented"
            w_in = od_w_in[j].astype(BF16)
            w_out = od_w_out[j].astype(BF16)
            pw = pool_scale.shape[1]
            vw = HEADS * HEAD_V
            qw = (w_in.shape[1] - pw - 2 * vw) // 2
            segs = [(pw, pw + qw, "rope", 0, ROPE_HEAD_DIM ** -0.5),
                    (pw + qw, pw + 2 * qw, "rope", 0, 1.0),
                    (0, pw, "copy", 0, 1.0),
                    (pw + 2 * qw, pw + 2 * qw + vw, "copy", 0, 1.0),
                    (pw + 2 * qw + vw, w_in.shape[1], "copy", 0, 1.0)]
            q_l, k_l, pv, v_l, gate = _in_proj(xs, mod_l, norm1_g[i], w_in, segs, [BF16] * 5, [], rope, seq, tm_lat,
                                               "in_odd_lat")
            w_kv = w_in[:, pw + qw:pw + 2 * qw + vw]
            k_c, v_c = _in_proj(cs, mod_c, norm1_g[i], w_kv, [(0, qw, "copy", 0, 1.0), (qw, qw + vw, "copy", 0, 1.0)],
                                [BF16, BF16], [], None, ctx_len, tm_ctx, "in_odd_ctx")
            o_l = _retention(ret_decay_f[j], ret_decay_b[j], q_l, k_l, v_l, k_c, v_c, seq, ctx_len, "retention")
            mix_params = [pool_w[j].astype(BF16), pool_scale[j].reshape(1, pw), ret_gn_g[j].reshape(1, HEAD_V)]
            xs = _mixer_ffn("odd", xs, [pv, gate, o_l], mod_l, w_out, mix_params, norm2_g[i], *ffn_w, seq, tm_ffn_lat,
                            fc, 1.0, "mix_ffn_odd_lat")
    return xs.reshape(batch, seq, d)
```

```python
import functools
import math

import jax
import jax.numpy as jnp
from jax import lax
from jax.experimental import pallas as pl
from jax.experimental.pallas import tpu as pltpu

F32 = jnp.float32
BF16 = jnp.bfloat16

EPS = 1e-6
GRID_W = 64
ROPE_HEAD_DIM = 64
ROPE_FREQS = ROPE_HEAD_DIM // 4
ROPE_BASE = 10000.0
HEADS = 4
HEAD_V = 128
RET_CHUNK = 128
POOL_RADII = (1, 2, 4, 8)
LANES = 128
HALO = 16
VMEM_LIMIT = 56 * 1024 * 1024
LOG2E = math.log2(math.e)
BOUND_MARGIN = 1.02
MAX_SAFE_SCORE_BOUND = 40.0
ATTN_ROWS = 1024


def _params(sem, vmem=VMEM_LIMIT):
    return pltpu.CompilerParams(dimension_semantics=sem, vmem_limit_bytes=vmem)


def _sigmoid(x):
    return 1.0 / (1.0 + jnp.exp(-x))


def _dot(a, b):
    return jnp.dot(a, b, preferred_element_type=F32)


def _dot_nt(a, b):
    return lax.dot_general(a, b, (((1,), (1,)), ((), ())), preferred_element_type=F32)


def _dot_tn(a, b):
    return lax.dot_general(a, b, (((0,), (0,)), ((), ())), preferred_element_type=F32)


def _rms(x, width):
    return x * lax.rsqrt(jnp.sum(x * x, axis=-1, keepdims=True) * (1.0 / width) + EPS)


def _norm_mod(x, g, shift, scale):
    return (_rms(x, x.shape[-1]) * g) * (1.0 + scale) + shift


def _half_mask(shape):
    return lax.broadcasted_iota(jnp.int32, shape, len(shape) - 1) < (LANES // 2)


def _rope(t, cos, sin):
    lane = lax.broadcasted_iota(jnp.int32, t.shape, 1)
    first = lax.rem(lane, 2 * ROPE_FREQS) < ROPE_FREQS
    partner = jnp.where(first, pltpu.roll(t, LANES - ROPE_FREQS, axis=1), pltpu.roll(t, ROPE_FREQS, axis=1))
    return t * cos + partner * sin


def _norm64(t, gain):
    lo = _half_mask(t.shape)
    sq = t * t
    s_lo = jnp.sum(jnp.where(lo, sq, 0.0), axis=-1, keepdims=True)
    s_hi = jnp.sum(jnp.where(lo, 0.0, sq), axis=-1, keepdims=True)
    w = 1.0 / (LANES // 2)
    r = jnp.where(lo, lax.rsqrt(s_lo * w + EPS), lax.rsqrt(s_hi * w + EPS))
    return (t * r) * gain


def _mod_kernel(c_ref, w_ref, b_ref, o_ref):
    c = c_ref[...]
    a = c * _sigmoid(c)
    o_ref[...] = _dot(a.astype(BF16), w_ref[...].astype(BF16)) + b_ref[...]


def _modulation(c_all, mod_w, mod_b):
    depth, d, n6 = mod_w.shape
    rows = c_all.shape[0]
    tn = 1536
    return pl.pallas_call(
        _mod_kernel,
        grid=(depth, n6 // tn),
        in_specs=[
            pl.BlockSpec((rows, d), lambda l, j: (0, 0)),
            pl.BlockSpec((None, d, tn), lambda l, j: (l, 0, j)),
            pl.BlockSpec((None, 1, tn), lambda l, j: (l, 0, j)),
        ],
        out_specs=pl.BlockSpec((None, rows, tn), lambda l, j: (l, 0, j)),
        out_shape=jax.ShapeDtypeStruct((depth, rows, n6), F32),
        compiler_params=_params(("parallel", "parallel")),
        name="modulation",
    )(c_all, mod_w, mod_b.reshape(depth, 1, n6))


def _in_proj_kernel(*refs, segs, rotate, n_gain):
    x_ref, mod_ref, g_ref, w_ref = refs[:4]
    gain_refs = refs[4:4 + n_gain]
    pos = 4 + n_gain
    if rotate:
        cos_ref, sin_ref = refs[pos:pos + 2]
        pos += 2
    out_refs = refs[pos:]
    h = _norm_mod(x_ref[...], g_ref[...], mod_ref[0:1, :], mod_ref[1:2, :]).astype(BF16)
    for (c0, c1, mode, gain_idx, scale), o_ref in zip(segs, out_refs):
        y = _dot(h, w_ref[:, c0:c1])
        if mode == "copy":
            o_ref[...] = y.astype(o_ref.dtype)
            continue
        for s in range((c1 - c0) // LANES):
            t = y[:, s * LANES:(s + 1) * LANES]
            if mode == "qknorm":
                t = _norm64(t, gain_refs[gain_idx][...])
            if rotate:
                t = _rope(t, cos_ref[...], sin_ref[...])
            if scale != 1.0:
                t = t * scale
            o_ref[:, s * LANES:(s + 1) * LANES] = t.astype(o_ref.dtype)


def _in_proj(x, mod, g, w, segs, out_dtypes, gains, rope, seq_len, tm, name):
    t_rows, d = x.shape
    per_seq = mod.shape[0] > 1
    tiles_per_seq = max(seq_len // tm, 1)
    if per_seq:
        assert seq_len % tm == 0
        mod_map = lambda i: (i // tiles_per_seq, 0, 0)
    else:
        mod_map = lambda i: (0, 0, 0)
    in_specs = [
        pl.BlockSpec((tm, d), lambda i: (i, 0)),
        pl.BlockSpec((None, 6, d), mod_map),
        pl.BlockSpec((1, d), lambda i: (0, 0)),
        pl.BlockSpec(w.shape, lambda i: (0, 0)),
    ]
    args = [x, mod, g.reshape(1, d), w]
    for gn in gains:
        in_specs.append(pl.BlockSpec((1, LANES), lambda i: (0, 0)))
        args.append(jnp.tile(gn.reshape(1, -1), (1, LANES // gn.shape[-1])))
    if rope is not None:
        assert seq_len % tm == 0
        for tab in rope:
            in_specs.append(pl.BlockSpec((tm, LANES), lambda i: (i % tiles_per_seq, 0)))
            args.append(tab)
    out_specs = [pl.BlockSpec((tm, c1 - c0), lambda i: (i, 0)) for (c0, c1, _, _, _) in segs]
    out_shape = [jax.ShapeDtypeStruct((t_rows, c1 - c0), dt) for (c0, c1, _, _, _), dt in zip(segs, out_dtypes)]
    return pl.pallas_call(
        functools.partial(_in_proj_kernel, segs=tuple(segs), rotate=rope is not None, n_gain=len(gains)),
        grid=(t_rows // tm,),
        in_specs=in_specs,
        out_specs=out_specs,
        out_shape=out_shape,
        compiler_params=_params(("parallel",)),
        name=name,
    )(*args)


def _diff_attn_kernel(*refs, n_src, lam_init):
    bound_ref, lam_ref, q_ref = refs[:3]
    kv_refs = refs[3:3 + 2 * n_src]
    o_ref = refs[3 + 2 * n_src]
    lv = lam_ref[...]
    lam = (jnp.exp(jnp.sum(lv[0:1] * lv[1:2], axis=-1, keepdims=True))
           - jnp.exp(jnp.sum(lv[2:3] * lv[3:4], axis=-1, keepdims=True)) + lam_init)
    bound = bound_ref[0]

    def attend_head(cols, use_bound):
        q = q_ref[:, cols]
        ks = [kv_refs[2 * j][:, cols] for j in range(n_src)]
        vs = [kv_refs[2 * j + 1][:, cols] for j in range(n_src)]

        def softmax_av(first_map):
            ss = []
            for k in ks:
                lo = _half_mask(k.shape)
                zero = jnp.zeros_like(k)
                ss.append(_dot_nt(jnp.where(lo, k, zero) if first_map else jnp.where(lo, zero, k), q))
            if use_bound:
                m = bound
            else:
                m = functools.reduce(jnp.maximum, [jnp.max(s, axis=0, keepdims=True) for s in ss])
            ps = [jnp.exp2(s - m) for s in ss]
            l = functools.reduce(jnp.add, [jnp.sum(p, axis=0, keepdims=True) for p in ps])
            acc = functools.reduce(jnp.add, [_dot_tn(v, p.astype(BF16)) for v, p in zip(vs, ps)])
            return acc, l

        acc0, l0 = softmax_av(True)
        acc1, l1 = softmax_av(False)
        o_t = acc0 * (1.0 / l0) - acc1 * (lam / l1)
        o_ref[:, cols] = o_t.T.astype(o_ref.dtype)

    def attend(use_bound):
        for hd in range(q_ref.shape[1] // HEAD_V):
            attend_head(slice(hd * HEAD_V, (hd + 1) * HEAD_V), use_bound)

    safe = bound <= MAX_SAFE_SCORE_BOUND

    @pl.when(safe)
    def _():
        attend(True)

    @pl.when(jnp.logical_not(safe))
    def _():
        attend(False)


def _diff_attn(bound, lam_vecs, q, kvs, seq_q, tq, heads_per_step, lam_init, name):
    t_rows, width = q.shape
    batch = t_rows // seq_q
    nq = seq_q // tq
    bw = heads_per_step * HEAD_V
    in_specs = [
        pl.BlockSpec(memory_space=pltpu.SMEM),
        pl.BlockSpec(lam_vecs.shape, lambda b, h, i: (0, 0)),
        pl.BlockSpec((tq, bw), lambda b, h, i: (b * nq + i, h)),
    ]
    args = [bound, lam_vecs, q]
    for k, v, seq_k in kvs:
        in_specs.append(pl.BlockSpec((seq_k, bw), lambda b, h, i: (b, h)))
        in_specs.append(pl.BlockSpec((seq_k, bw), lambda b, h, i: (b, h)))
        args += [k, v]
    return pl.pallas_call(
        functools.partial(_diff_attn_kernel, n_src=len(kvs), lam_init=lam_init),
        grid=(batch, width // bw, nq),
        in_specs=in_specs,
        out_specs=pl.BlockSpec((tq, bw), lambda b, h, i: (b * nq + i, h)),
        out_shape=jax.ShapeDtypeStruct((t_rows, width), BF16),
        compiler_params=_params(("parallel", "parallel", "parallel")),
        name=name,
    )(*args)


def _three(arr, tm):
    t_rows, width = arr.shape
    per = tm // HALO
    last = t_rows // HALO - 1
    prev = pl.BlockSpec((HALO, width), lambda i: (jnp.maximum(i * per - 1, 0), 0))
    nxt = pl.BlockSpec((HALO, width), lambda i: (jnp.minimum((i + 1) * per, last), 0))
    return [(arr, prev), (arr, pl.BlockSpec((tm, width), lambda i: (i, 0))), (arr, nxt)]


def _cat_rows(refs):
    return jnp.concatenate([r[...] for r in refs], axis=0)


def _tile_rows(tm, seq_len):
    row = lax.broadcasted_iota(jnp.int32, (tm + 2 * HALO, 1), 0)
    if tm > seq_len:
        return lax.rem(pl.program_id(0) * tm + (seq_len - HALO) + row, seq_len), None
    pos = lax.rem(pl.program_id(0) * tm, seq_len) - HALO + row
    return pos, jnp.logical_and(pos >= 0, pos < seq_len).astype(F32)


def _shift_rows(u, pos, own, seq_len):
    rows = u.shape[0]
    dn = pltpu.roll(u, 1, axis=0)
    up = pltpu.roll(u, rows - 1, axis=0)
    if own is None:
        dn = jnp.where(pos == 0, 0.0, dn)
        up = jnp.where(pos == seq_len - 1, 0.0, up)
    return dn, up


def _head_rms(o, gain, post_scale=1.0):
    parts = []
    for h in range(o.shape[-1] // HEAD_V):
        t = _rms(o[:, h * HEAD_V:(h + 1) * HEAD_V].astype(F32), HEAD_V) * gain
        parts.append(t * post_scale if post_scale != 1.0 else t)
    return parts


def _even_mix(gate_refs, o_refs, cw_ref, sg_ref, pos, own, seq_len, post_scale):
    gates = _cat_rows(gate_refs).astype(F32)
    w = gates.shape[1] // 3
    u = gates[:, w:2 * w] * gates[:, 2 * w:3 * w]
    if own is not None:
        u = u * own
    dn, up = _shift_rows(u, pos, own, seq_len)
    y_conv = gates[:, 0:w] * (dn * cw_ref[0:1, :] + u * cw_ref[1:2, :] + up * cw_ref[2:3, :])
    heads = _head_rms(_cat_rows(o_refs), sg_ref[...], post_scale)
    return jnp.concatenate([y_conv.astype(BF16)] + [p.astype(BF16) for p in heads], axis=-1)


def _odd_mix(pv_refs, gate_refs, o_refs, pw_ref, ps_ref, gg_ref, pos, own, seq_len):
    assert own is not None and max(POOL_RADII) < HALO
    pv = _cat_rows(pv_refs).astype(F32) * own
    rows = pv.shape[0]
    parts = []
    for gi, r in enumerate(POOL_RADII):
        v = pv[:, gi * LANES:(gi + 1) * LANES]
        win = v
        span = 1
        while span < 2 * r:
            win = win + pltpu.roll(win, rows - span, axis=0)
            span *= 2
        total = pltpu.roll(win, r, axis=0) + pltpu.roll(v, rows - r, axis=0)
        cnt = jnp.maximum(jnp.minimum(pos + r + 1, seq_len) - jnp.maximum(pos - r, 0), 1).astype(F32)
        y = (total / cnt - v).astype(BF16)
        parts.append(_dot(y, pw_ref[gi]))
    y_p = jnp.concatenate(parts, axis=-1) * ps_ref[...]
    gate = _cat_rows(gate_refs).astype(F32)
    y_r = jnp.concatenate(_head_rms(_cat_rows(o_refs), gg_ref[...]), axis=-1) * (gate * _sigmoid(gate))
    return jnp.concatenate([y_p.astype(BF16), y_r.astype(BF16)], axis=-1)


def _mixer_ffn_kernel(*refs, kind, n_mix, seq_len, fc, post_scale):
    x_refs = refs[0:3]
    mix_refs = [refs[3 + 3 * k:6 + 3 * k] for k in range(n_mix)]
    p = 3 + 3 * n_mix
    mod_ref, n2_ref, wo_ref = refs[p:p + 3]
    n_par = 2 if kind == "even" else 3
    mix_par = refs[p + 3:p + 3 + n_par]
    wu_ref, cw_ref, cb_ref, wd_ref, y_ref, h_ref, act_ref = refs[p + 3 + n_par:]
    tm = y_ref.shape[0]
    rows = tm + 2 * HALO
    fdim = wd_ref.shape[0]
    pos, own = _tile_rows(tm, seq_len)

    if kind == "even":
        z = _even_mix(mix_refs[0], mix_refs[1], mix_par[0], mix_par[1], pos, own, seq_len, post_scale)
    else:
        z = _odd_mix(mix_refs[0], mix_refs[1], mix_refs[2], mix_par[0], mix_par[1], mix_par[2], pos, own, seq_len)
    x1 = _cat_rows(x_refs) + mod_ref[2:3, :] * _dot(z, wo_ref[...])
    y_ref[...] = x1[HALO:HALO + tm]
    h = _norm_mod(x1, n2_ref[...], mod_ref[3:4, :], mod_ref[4:5, :])
    if own is not None:
        h = h * own
    h_ref[...] = h.astype(BF16)

    for j in range(fdim // fc):
        cols = slice(j * fc, (j + 1) * fc)
        a_ext = _dot(h_ref[...], wu_ref[:, cols])
        dn = pltpu.roll(a_ext, 1, axis=0)[HALO:HALO + tm]
        up = pltpu.roll(a_ext, rows - 1, axis=0)[HALO:HALO + tm]
        if own is None:
            dn = jnp.where(pos[HALO:HALO + tm] == 0, 0.0, dn)
            up = jnp.where(pos[HALO:HALO + tm] == seq_len - 1, 0.0, up)
        a = (dn * cw_ref[0:1, cols] + a_ext[HALO:HALO + tm] * cw_ref[1:2, cols] + up * cw_ref[2:3, cols]
             + cb_ref[:, cols])
        b = _dot(h_ref[HALO:HALO + tm, :], wu_ref[:, fdim + j * fc:fdim + (j + 1) * fc])
        act_ref[:, cols] = (a * _sigmoid(a) * b).astype(BF16)
    y_ref[...] += mod_ref[5:6, :] * _dot(act_ref[...], wd_ref[...])


def _mixer_ffn(kind, x, mix_arrays, mod, w_out, mix_params, norm_g, w_up, conv_w, conv_b, w_down, seq_len, tm, fc,
               post_scale, name):
    t_rows, d = x.shape
    fdim = w_down.shape[0]
    per_seq = mod.shape[0] > 1
    assert tm % HALO == 0 and (seq_len % tm == 0 or tm % seq_len == 0)
    assert seq_len % tm == 0 or not per_seq
    tiles_per_seq = max(seq_len // tm, 1)
    mod_map = (lambda i: (i // tiles_per_seq, 0, 0)) if per_seq else (lambda i: (0, 0, 0))

    def whole(arr, single=False):
        mode = dict(pipeline_mode=pl.Buffered(1)) if single else {}
        return arr, pl.BlockSpec(arr.shape, lambda i: (0,) * arr.ndim, **mode)

    operands = _three(x, tm)
    for arr in mix_arrays:
        operands += _three(arr, tm)
    operands += [(mod, pl.BlockSpec((None, 6, d), mod_map)), whole(norm_g.reshape(1, d)), whole(w_out, True)]
    operands += [whole(a) for a in mix_params]
    operands += [whole(w_up, True), whole(conv_w), whole(conv_b.reshape(1, fdim)), whole(w_down, True)]
    return pl.pallas_call(
        functools.partial(_mixer_ffn_kernel, kind=kind, n_mix=len(mix_arrays), seq_len=seq_len, fc=fc,
                          post_scale=post_scale),
        grid=(t_rows // tm,),
        in_specs=[spec for _, spec in operands],
        out_specs=pl.BlockSpec((tm, d), lambda i: (i, 0)),
        out_shape=jax.ShapeDtypeStruct((t_rows, d), F32),
        scratch_shapes=[pltpu.VMEM((tm + 2 * HALO, d), BF16),
                        pltpu.VMEM((tm, fdim), BF16)],
        compiler_params=_params(("parallel",)),
        name=name,
    )(*[arr for arr, _ in operands])


def _retention_kernel(df_ref, db_ref, q_ref, k_ref, v_ref, kc_ref, vc_ref, o_ref, kv_ref, ss_ref):
    n = q_ref.shape[0]
    lc = kc_ref.shape[0]
    c = RET_CHUNK
    nc = n // c
    pos = lax.broadcasted_iota(jnp.int32, (c, 1), 0).astype(F32)
    ii = lax.broadcasted_iota(jnp.int32, (c, c), 0)
    jj = lax.broadcasted_iota(jnp.int32, (c, c), 1)
    dij = (ii - jj).astype(F32)
    jpos = lax.broadcasted_iota(jnp.int32, (lc, 1), 0).astype(F32)
    lo = _half_mask((1, LANES))
    row_lo = lax.broadcasted_iota(jnp.int32, (LANES, 1), 0) < (LANES // 2)
    block_diag = (lax.broadcasted_iota(jnp.int32, (LANES, 2 * HEAD_V), 0) // (LANES // 2)
                  == lax.broadcasted_iota(jnp.int32, (LANES, 2 * HEAD_V), 1) // HEAD_V)
    zeros_v = jnp.zeros((c, HEAD_V), BF16)

    def cat(a, b):
        return jnp.concatenate([a, b], axis=-1)

    for pair in range(HEADS // 2):
        slab = slice(pair * LANES, (pair + 1) * LANES)
        vcol = slice(pair * 2 * HEAD_V, (pair + 1) * 2 * HEAD_V)
        lg = []
        for d_ref in (df_ref, db_ref):
            lg.append([-jnp.exp(jnp.full((1, 1), d_ref[2 * pair + e], F32)) for e in range(2)])
        (lgf0, lgf1), (lgb0, lgb1) = lg
        lgf_lane = jnp.where(lo, lgf0, lgf1)
        lgb_lane = jnp.where(lo, lgb0, lgb1)
        lgf_row = jnp.where(row_lo, lgf0, lgf1)
        lgb_row = jnp.where(row_lo, lgb0, lgb1)
        qdec = cat(jnp.exp(lgf_lane * (pos + 1.0)), jnp.exp(lgb_lane * (c - pos)))
        kdec = cat(jnp.exp(lgf_lane * (c - 1.0 - pos)), jnp.exp(lgb_lane * pos))
        cdec_f = jnp.exp(lgf_row * float(c))
        cdec_b = jnp.exp(lgb_row * float(c))
        intra = [jnp.where(dij >= 0, jnp.exp(f * jnp.maximum(dij, 0.0)), 0.0)
                 + jnp.where(dij <= 0, jnp.exp(b * jnp.maximum(-dij, 0.0)), 0.0)
                 for f, b in ((lgf0, lgb0), (lgf1, lgb1))]

        for ci in range(nc):
            rows = slice(ci * c, (ci + 1) * c)
            k = k_ref[rows, slab].astype(F32)
            kv_ref[ci] = _dot_tn((cat(k, k) * kdec).astype(BF16), v_ref[rows, vcol].astype(BF16))
        kc = kc_ref[:, slab].astype(F32)
        cdec0 = cat(jnp.exp(lgf_lane * (lc - 1.0 - jpos)), jnp.exp(lgb_lane * jpos))
        s0 = _dot_tn((cat(kc, kc) * cdec0).astype(BF16), vc_ref[:, vcol].astype(BF16))

        s = s0[0:LANES]
        for ci in range(nc):
            ss_ref[ci, 0:LANES, :] = jnp.where(block_diag, s, 0.0).astype(BF16)
            s = s * cdec_f + kv_ref[ci, 0:LANES, :]
        s = s0[LANES:2 * LANES]
        for ci in reversed(range(nc)):
            ss_ref[ci, LANES:2 * LANES, :] = jnp.where(block_diag, s, 0.0).astype(BF16)
            s = s * cdec_b + kv_ref[ci, LANES:2 * LANES, :]

        for ci in range(nc):
            rows = slice(ci * c, (ci + 1) * c)
            q = q_ref[rows, slab].astype(F32)
            kb = k_ref[rows, slab].astype(BF16)
            v = v_ref[rows, vcol].astype(BF16)
            att0 = _dot_nt(jnp.where(lo, q, 0.0).astype(BF16), kb) * intra[0]
            att1 = _dot_nt(jnp.where(lo, 0.0, q).astype(BF16), kb) * intra[1]
            v_blocks = jnp.concatenate([cat(v[:, 0:HEAD_V], zeros_v), cat(zeros_v, v[:, HEAD_V:])], axis=0)
            o_ref[rows, vcol] = (_dot(cat(att0, att1).astype(BF16), v_blocks)
                                 + _dot((cat(q, q) * qdec).astype(BF16), ss_ref[ci])).astype(o_ref.dtype)


def _retention(dec_f, dec_b, q, k, v, kc, vc, seq_len, ctx_len, name):
    t_rows = q.shape[0]
    batch = t_rows // seq_len
    qw, vw = q.shape[1], v.shape[1]
    nc = seq_len // RET_CHUNK
    smem = pl.BlockSpec(memory_space=pltpu.SMEM)
    return pl.pallas_call(
        _retention_kernel,
        grid=(batch,),
        in_specs=[
            smem, smem,
            pl.BlockSpec((seq_len, qw), lambda b: (b, 0)),
            pl.BlockSpec((seq_len, qw), lambda b: (b, 0)),
            pl.BlockSpec((seq_len, vw), lambda b: (b, 0)),
            pl.BlockSpec((ctx_len, qw), lambda b: (b, 0)),
            pl.BlockSpec((ctx_len, vw), lambda b: (b, 0)),
        ],
        out_specs=pl.BlockSpec((seq_len, vw), lambda b: (b, 0)),
        out_shape=jax.ShapeDtypeStruct((t_rows, vw), BF16),
        scratch_shapes=[pltpu.VMEM((nc, 2 * LANES, 2 * HEAD_V), F32),
                        pltpu.VMEM((nc, 2 * LANES, 2 * HEAD_V), BF16)],
        compiler_params=_params(("parallel",)),
        name=name,
    )(dec_f, dec_b, q, k, v, kc, vc)


def _rope_tables(seq_len):
    rows = seq_len // GRID_W
    row = jnp.repeat(jnp.arange(rows, dtype=F32), GRID_W)
    col = jnp.tile(jnp.arange(GRID_W, dtype=F32), rows)
    inv = ROPE_BASE ** (-jnp.arange(ROPE_FREQS, dtype=F32) / ROPE_FREQS)
    ar, ac = row[:, None] * inv, col[:, None] * inv
    cos = jnp.concatenate([jnp.cos(ar), jnp.cos(ar), jnp.cos(ac), jnp.cos(ac)], axis=-1)
    sin = jnp.concatenate([-jnp.sin(ar), jnp.sin(ar), -jnp.sin(ac), jnp.sin(ac)], axis=-1)
    reps = LANES // ROPE_HEAD_DIM
    return jnp.tile(cos, (1, reps)), jnp.tile(sin, (1, reps))


def _tile(total, want):
    t = min(want, total)
    assert total % t == 0
    return t


def kernel(x, c, ctx, c_ctx, mod_w, mod_b, norm1_g, norm2_g, ev_w_in, ev_w_out, sc_conv_w, da_q_norm, da_k_norm,
           da_lq1, da_lk1, da_lq2, da_lk2, da_subln_g, od_w_in, od_w_out, pool_w, pool_scale, ret_decay_f,
           ret_decay_b, ret_gn_g, ffn_w_up, ffn_conv_w, ffn_conv_b, ffn_w_down):
    batch, seq, d = x.shape
    ctx_len = ctx.shape[1]
    depth = mod_w.shape[0]
    t_lat, t_ctx = batch * seq, batch * ctx_len
    xs = x.reshape(t_lat, d)
    cs = ctx.reshape(t_ctx, d)

    pad = (-(batch + 1)) % 8
    c_all = jnp.concatenate([c, c_ctx[None, :], jnp.zeros((pad, d), F32)], axis=0)
    mods = _modulation(c_all, mod_w, mod_b)
    rope = _rope_tables(seq)

    tm_lat = _tile(seq, 512)
    tm_ctx = _tile(t_ctx, 512)
    tm_ffn_lat = _tile(seq, 1024)
    tm_ffn_ctx = _tile(t_ctx, 512)
    tq = _tile(seq, ATTN_ROWS)
    fdim = ffn_w_down.shape[1]
    fc = 256 if fdim % 256 == 0 else LANES

    for i in range(depth):
        last = i == depth - 1
        j = i // 2
        mod_l = mods[i, :batch].reshape(batch, 6, d)
        mod_c = mods[i, batch:batch + 1].reshape(1, 6, d)
        ffn_w = (ffn_w_up[i].astype(BF16), ffn_conv_w[i], ffn_conv_b[i], ffn_w_down[i].astype(BF16))
        if i % 2 == 0:
            w_in = ev_w_in[j].astype(BF16)
            w_out = ev_w_out[j].astype(BF16)
            scw = ev_w_out.shape[1] // 2
            qkw = (w_in.shape[1] - 4 * scw) // 2
            q_scale = ROPE_HEAD_DIM ** -0.5 * LOG2E
            segs = [(3 * scw, 3 * scw + qkw, "qknorm", 0, q_scale),
                    (3 * scw + qkw, 3 * scw + 2 * qkw, "qknorm", 1, 1.0),
                    (0, 3 * scw, "copy", 0, 1.0),
                    (3 * scw + 2 * qkw, w_in.shape[1], "copy", 0, 1.0)]
            dts = [BF16, BF16, BF16, BF16]
            gains = [da_q_norm[j], da_k_norm[j]]
            lam_init = 0.8 - 0.6 * math.exp(-0.3 * i)
            lam_vecs = jnp.stack([da_lq1[j], da_lk1[j], da_lq2[j], da_lk2[j]], axis=0)
            bound = (ROPE_HEAD_DIM * q_scale * BOUND_MARGIN * jnp.max(jnp.abs(da_q_norm[j]))
                     * jnp.max(jnp.abs(da_k_norm[j]))).reshape(1).astype(F32)
            q_l, k_l, g_l, v_l = _in_proj(xs, mod_l, norm1_g[i], w_in, segs, dts, gains, rope, seq, tm_lat,
                                          "in_even_lat")
            q_c, k_c, g_c, v_c = _in_proj(cs, mod_c, norm1_g[i], w_in, segs, dts, gains, None, ctx_len, tm_ctx,
                                          "in_even_ctx")
            o_l = _diff_attn(bound, lam_vecs, q_l, [(k_l, v_l, seq), (k_c, v_c, ctx_len)], seq, tq, 1, lam_init,
                             "attn_lat")
            mix_params = [sc_conv_w[j], da_subln_g[j].reshape(1, HEAD_V)]
            xs = _mixer_ffn("even", xs, [g_l, o_l], mod_l, w_out, mix_params, norm2_g[i], *ffn_w, seq, tm_ffn_lat, fc,
                            1.0 - lam_init, "mix_ffn_even_lat")
            if not last:
                o_c = _diff_attn(bound, lam_vecs, q_c, [(k_c, v_c, ctx_len)], ctx_len, ctx_len, HEADS, lam_init,
                                 "attn_ctx")
                cs = _mixer_ffn("even", cs, [g_c, o_c], mod_c, w_out, mix_params, norm2_g[i], *ffn_w, ctx_len,
                                tm_ffn_ctx, fc, 1.0 - lam_init, "mix_ffn_even_ctx")
        else:
            assert last, "an odd layer that still updates the context stream is not implemented"
            w_in = od_w_in[j].astype(BF16)
            w_out = od_w_out[j].astype(BF16)
            pw = pool_scale.shape[1]
            vw = HEADS * HEAD_V
            qw = (w_in.shape[1] - pw - 2 * vw) // 2
            segs = [(pw, pw + qw, "rope", 0, ROPE_HEAD_DIM ** -0.5),
                    (pw + qw, pw + 2 * qw, "rope", 0, 1.0),
                    (0, pw, "copy", 0, 1.0),
                    (pw + 2 * qw, pw + 2 * qw + vw, "copy", 0, 1.0),
                    (pw + 2 * qw + vw, w_in.shape[1], "copy", 0, 1.0)]
            q_l, k_l, pv, v_l, gate = _in_proj(xs, mod_l, norm1_g[i], w_in, segs, [BF16] * 5, [], rope, seq, tm_lat,
                                               "in_odd_lat")
            w_kv = w_in[:, pw + qw:pw + 2 * qw + vw]
            k_c, v_c = _in_proj(cs, mod_c, norm1_g[i], w_kv, [(0, qw, "copy", 0, 1.0), (qw, qw + vw, "copy", 0, 1.0)],
                                [BF16, BF16], [], None, ctx_len, tm_ctx, "in_odd_ctx")
            o_l = _retention(ret_decay_f[j], ret_decay_b[j], q_l, k_l, v_l, k_c, v_c, seq, ctx_len, "retention")
            mix_params = [pool_w[j].astype(BF16), pool_scale[j].reshape(1, pw), ret_gn_g[j].reshape(1, HEAD_V)]
            xs = _mixer_ffn("odd", xs, [pv, gate, o_l], mod_l, w_out, mix_params, norm2_g[i], *ffn_w, seq, tm_ffn_lat,
                            fc, 1.0, "mix_ffn_odd_lat")
    return xs.reshape(batch, seq, d)
```

```python
import functools
import math

import jax
import jax.numpy as jnp
from jax import lax
from jax.experimental import pallas as pl
from jax.experimental.pallas import tpu as pltpu

F32 = jnp.float32
BF16 = jnp.bfloat16

EPS = 1e-6
GRID_W = 64
ROPE_HEAD_DIM = 64
ROPE_FREQS = ROPE_HEAD_DIM // 4
ROPE_BASE = 10000.0
HEADS = 4
HEAD_V = 128
RET_CHUNK = 128
POOL_RADII = (1, 2, 4, 8)
LANES = 128
HALO = 16
VMEM_LIMIT = 56 * 1024 * 1024
LOG2E = math.log2(math.e)
BOUND_MARGIN = 1.02
MAX_SAFE_SCORE_BOUND = 40.0
ATTN_ROWS = 2048


def _params(sem, vmem=VMEM_LIMIT):
    return pltpu.CompilerParams(dimension_semantics=sem, vmem_limit_bytes=vmem)


def _sigmoid(x):
    return 1.0 / (1.0 + jnp.exp(-x))


def _dot(a, b):
    return jnp.dot(a, b, preferred_element_type=F32)


def _dot_nt(a, b):
    return lax.dot_general(a, b, (((1,), (1,)), ((), ())), preferred_element_type=F32)


def _dot_tn(a, b):
    return lax.dot_general(a, b, (((0,), (0,)), ((), ())), preferred_element_type=F32)


def _rms(x, width):
    return x * lax.rsqrt(jnp.sum(x * x, axis=-1, keepdims=True) * (1.0 / width) + EPS)


def _norm_mod(x, g, shift, scale):
    return (_rms(x, x.shape[-1]) * g) * (1.0 + scale) + shift


def _half_mask(shape):
    return lax.broadcasted_iota(jnp.int32, shape, len(shape) - 1) < (LANES // 2)


def _rope(t, cos, sin):
    lane = lax.broadcasted_iota(jnp.int32, t.shape, 1)
    first = lax.rem(lane, 2 * ROPE_FREQS) < ROPE_FREQS
    partner = jnp.where(first, pltpu.roll(t, LANES - ROPE_FREQS, axis=1), pltpu.roll(t, ROPE_FREQS, axis=1))
    return t * cos + partner * sin


def _norm64(t, gain):
    lo = _half_mask(t.shape)
    sq = t * t
    s_lo = jnp.sum(jnp.where(lo, sq, 0.0), axis=-1, keepdims=True)
    s_hi = jnp.sum(jnp.where(lo, 0.0, sq), axis=-1, keepdims=True)
    w = 1.0 / (LANES // 2)
    r = jnp.where(lo, lax.rsqrt(s_lo * w + EPS), lax.rsqrt(s_hi * w + EPS))
    return (t * r) * gain


def _mod_kernel(c_ref, w_ref, b_ref, o_ref):
    c = c_ref[...]
    a = c * _sigmoid(c)
    o_ref[...] = _dot(a.astype(BF16), w_ref[...].astype(BF16)) + b_ref[...]


def _modulation(c_all, mod_w, mod_b):
    depth, d, n6 = mod_w.shape
    rows = c_all.shape[0]
    tn = 1536
    return pl.pallas_call(
        _mod_kernel,
        grid=(depth, n6 // tn),
        in_specs=[
            pl.BlockSpec((rows, d), lambda l, j: (0, 0)),
            pl.BlockSpec((None, d, tn), lambda l, j: (l, 0, j)),
            pl.BlockSpec((None, 1, tn), lambda l, j: (l, 0, j)),
        ],
        out_specs=pl.BlockSpec((None, rows, tn), lambda l, j: (l, 0, j)),
        out_shape=jax.ShapeDtypeStruct((depth, rows, n6), F32),
        compiler_params=_params(("parallel", "parallel")),
        name="modulation",
    )(c_all, mod_w, mod_b.reshape(depth, 1, n6))


def _in_proj_kernel(*refs, segs, rotate, n_gain):
    x_ref, mod_ref, g_ref, w_ref = refs[:4]
    gain_refs = refs[4:4 + n_gain]
    pos = 4 + n_gain
    if rotate:
        cos_ref, sin_ref = refs[pos:pos + 2]
        pos += 2
    out_refs = refs[pos:]
    h = _norm_mod(x_ref[...], g_ref[...], mod_ref[0:1, :], mod_ref[1:2, :]).astype(BF16)
    for (c0, c1, mode, gain_idx, scale), o_ref in zip(segs, out_refs):
        y = _dot(h, w_ref[:, c0:c1])
        if mode == "copy":
            o_ref[...] = y.astype(o_ref.dtype)
            continue
        for s in range((c1 - c0) // LANES):
            t = y[:, s * LANES:(s + 1) * LANES]
            if mode == "qknorm":
                t = _norm64(t, gain_refs[gain_idx][...])
            if rotate:
                t = _rope(t, cos_ref[...], sin_ref[...])
            if scale != 1.0:
                t = t * scale
            o_ref[:, s * LANES:(s + 1) * LANES] = t.astype(o_ref.dtype)


def _in_proj(x, mod, g, w, segs, out_dtypes, gains, rope, seq_len, tm, name):
    t_rows, d = x.shape
    per_seq = mod.shape[0] > 1
    tiles_per_seq = max(seq_len // tm, 1)
    if per_seq:
        assert seq_len % tm == 0
        mod_map = lambda i: (i // tiles_per_seq, 0, 0)
    else:
        mod_map = lambda i: (0, 0, 0)
    in_specs = [
        pl.BlockSpec((tm, d), lambda i: (i, 0)),
        pl.BlockSpec((None, 6, d), mod_map),
        pl.BlockSpec((1, d), lambda i: (0, 0)),
        pl.BlockSpec(w.shape, lambda i: (0, 0)),
    ]
    args = [x, mod, g.reshape(1, d), w]
    for gn in gains:
        in_specs.append(pl.BlockSpec((1, LANES), lambda i: (0, 0)))
        args.append(jnp.tile(gn.reshape(1, -1), (1, LANES // gn.shape[-1])))
    if rope is not None:
        assert seq_len % tm == 0
        for tab in rope:
            in_specs.append(pl.BlockSpec((tm, LANES), lambda i: (i % tiles_per_seq, 0)))
            args.append(tab)
    out_specs = [pl.BlockSpec((tm, c1 - c0), lambda i: (i, 0)) for (c0, c1, _, _, _) in segs]
    out_shape = [jax.ShapeDtypeStruct((t_rows, c1 - c0), dt) for (c0, c1, _, _, _), dt in zip(segs, out_dtypes)]
    return pl.pallas_call(
        functools.partial(_in_proj_kernel, segs=tuple(segs), rotate=rope is not None, n_gain=len(gains)),
        grid=(t_rows // tm,),
        in_specs=in_specs,
        out_specs=out_specs,
        out_shape=out_shape,
        compiler_params=_params(("parallel",)),
        name=name,
    )(*args)


def _diff_attn_kernel(*refs, n_src, lam_init):
    bound_ref, lam_ref, q_ref = refs[:3]
    kv_refs = refs[3:3 + 2 * n_src]
    o_ref = refs[3 + 2 * n_src]
    lv = lam_ref[...]
    lam = (jnp.exp(jnp.sum(lv[0:1] * lv[1:2], axis=-1, keepdims=True))
           - jnp.exp(jnp.sum(lv[2:3] * lv[3:4], axis=-1, keepdims=True)) + lam_init)
    bound = bound_ref[0]

    def attend_head(cols, use_bound):
        q = q_ref[:, cols]
        ks = [kv_refs[2 * j][:, cols] for j in range(n_src)]
        vs = [kv_refs[2 * j + 1][:, cols] for j in range(n_src)]

        def softmax_av(first_map):
            ss = []
            for k in ks:
                lo = _half_mask(k.shape)
                zero = jnp.zeros_like(k)
                ss.append(_dot_nt(jnp.where(lo, k, zero) if first_map else jnp.where(lo, zero, k), q))
            if use_bound:
                m = bound
            else:
                m = functools.reduce(jnp.maximum, [jnp.max(s, axis=0, keepdims=True) for s in ss])
            ps = [jnp.exp2(s - m) for s in ss]
            l = functools.reduce(jnp.add, [jnp.sum(p, axis=0, keepdims=True) for p in ps])
            acc = functools.reduce(jnp.add, [_dot_tn(v, p.astype(BF16)) for v, p in zip(vs, ps)])
            return acc, l

        acc0, l0 = softmax_av(True)
        acc1, l1 = softmax_av(False)
        o_t = acc0 * (1.0 / l0) - acc1 * (lam / l1)
        o_ref[:, cols] = o_t.T.astype(o_ref.dtype)

    def attend(use_bound):
        for hd in range(q_ref.shape[1] // HEAD_V):
            attend_head(slice(hd * HEAD_V, (hd + 1) * HEAD_V), use_bound)

    safe = bound <= MAX_SAFE_SCORE_BOUND

    @pl.when(safe)
    def _():
        attend(True)

    @pl.when(jnp.logical_not(safe))
    def _():
        attend(False)


def _diff_attn(bound, lam_vecs, q, kvs, seq_q, tq, heads_per_step, lam_init, name):
    t_rows, width = q.shape
    batch = t_rows // seq_q
    nq = seq_q // tq
    bw = heads_per_step * HEAD_V
    in_specs = [
        pl.BlockSpec(memory_space=pltpu.SMEM),
        pl.BlockSpec(lam_vecs.shape, lambda b, h, i: (0, 0)),
        pl.BlockSpec((tq, bw), lambda b, h, i: (b * nq + i, h)),
    ]
    args = [bound, lam_vecs, q]
    for k, v, seq_k in kvs:
        in_specs.append(pl.BlockSpec((seq_k, bw), lambda b, h, i: (b, h)))
        in_specs.append(pl.BlockSpec((seq_k, bw), lambda b, h, i: (b, h)))
        args += [k, v]
    return pl.pallas_call(
        functools.partial(_diff_attn_kernel, n_src=len(kvs), lam_init=lam_init),
        grid=(batch, width // bw, nq),
        in_specs=in_specs,
        out_specs=pl.BlockSpec((tq, bw), lambda b, h, i: (b * nq + i, h)),
        out_shape=jax.ShapeDtypeStruct((t_rows, width), BF16),
        compiler_params=_params(("parallel", "parallel", "parallel")),
        name=name,
    )(*args)


def _three(arr, tm):
    t_rows, width = arr.shape
    per = tm // HALO
    last = t_rows // HALO - 1
    prev = pl.BlockSpec((HALO, width), lambda i: (jnp.maximum(i * per - 1, 0), 0))
    nxt = pl.BlockSpec((HALO, width), lambda i: (jnp.minimum((i + 1) * per, last), 0))
    return [(arr, prev), (arr, pl.BlockSpec((tm, width), lambda i: (i, 0))), (arr, nxt)]


def _cat_rows(refs):
    return jnp.concatenate([r[...] for r in refs], axis=0)


def _tile_rows(tm, seq_len):
    row = lax.broadcasted_iota(jnp.int32, (tm + 2 * HALO, 1), 0)
    if tm > seq_len:
        return lax.rem(pl.program_id(0) * tm + (seq_len - HALO) + row, seq_len), None
    pos = lax.rem(pl.program_id(0) * tm, seq_len) - HALO + row
    return pos, jnp.logical_and(pos >= 0, pos < seq_len).astype(F32)


def _shift_rows(u, pos, own, seq_len):
    rows = u.shape[0]
    dn = pltpu.roll(u, 1, axis=0)
    up = pltpu.roll(u, rows - 1, axis=0)
    if own is None:
        dn = jnp.where(pos == 0, 0.0, dn)
        up = jnp.where(pos == seq_len - 1, 0.0, up)
    return dn, up


def _head_rms(o, gain, post_scale=1.0):
    parts = []
    for h in range(o.shape[-1] // HEAD_V):
        t = _rms(o[:, h * HEAD_V:(h + 1) * HEAD_V].astype(F32), HEAD_V) * gain
        parts.append(t * post_scale if post_scale != 1.0 else t)
    return parts


def _even_mix(gate_refs, o_refs, cw_ref, sg_ref, pos, own, seq_len, post_scale):
    gates = _cat_rows(gate_refs).astype(F32)
    w = gates.shape[1] // 3
    u = gates[:, w:2 * w] * gates[:, 2 * w:3 * w]
    if own is not None:
        u = u * own
    dn, up = _shift_rows(u, pos, own, seq_len)
    y_conv = gates[:, 0:w] * (dn * cw_ref[0:1, :] + u * cw_ref[1:2, :] + up * cw_ref[2:3, :])
    heads = _head_rms(_cat_rows(o_refs), sg_ref[...], post_scale)
    return jnp.concatenate([y_conv.astype(BF16)] + [p.astype(BF16) for p in heads], axis=-1)


def _odd_mix(pv_refs, gate_refs, o_refs, pw_ref, ps_ref, gg_ref, pos, own, seq_len):
    assert own is not None and max(POOL_RADII) < HALO
    pv = _cat_rows(pv_refs).astype(F32) * own
    rows = pv.shape[0]
    parts = []
    for gi, r in enumerate(POOL_RADII):
        v = pv[:, gi * LANES:(gi + 1) * LANES]
        win = v
        span = 1
        while span < 2 * r:
            win = win + pltpu.roll(win, rows - span, axis=0)
            span *= 2
        total = pltpu.roll(win, r, axis=0) + pltpu.roll(v, rows - r, axis=0)
        cnt = jnp.maximum(jnp.minimum(pos + r + 1, seq_len) - jnp.maximum(pos - r, 0), 1).astype(F32)
        y = (total / cnt - v).astype(BF16)
        parts.append(_dot(y, pw_ref[gi]))
    y_p = jnp.concatenate(parts, axis=-1) * ps_ref[...]
    gate = _cat_rows(gate_refs).astype(F32)
    y_r = jnp.concatenate(_head_rms(_cat_rows(o_refs), gg_ref[...]), axis=-1) * (gate * _sigmoid(gate))
    return jnp.concatenate([y_p.astype(BF16), y_r.astype(BF16)], axis=-1)


def _mixer_ffn_kernel(*refs, kind, n_mix, seq_len, fc, post_scale):
    x_refs = refs[0:3]
    mix_refs = [refs[3 + 3 * k:6 + 3 * k] for k in range(n_mix)]
    p = 3 + 3 * n_mix
    mod_ref, n2_ref, wo_ref = refs[p:p + 3]
    n_par = 2 if kind == "even" else 3
    mix_par = refs[p + 3:p + 3 + n_par]
    wu_ref, cw_ref, cb_ref, wd_ref, y_ref, h_ref, act_ref = refs[p + 3 + n_par:]
    tm = y_ref.shape[0]
    rows = tm + 2 * HALO
    fdim = wd_ref.shape[0]
    pos, own = _tile_rows(tm, seq_len)

    if kind == "even":
        z = _even_mix(mix_refs[0], mix_refs[1], mix_par[0], mix_par[1], pos, own, seq_len, post_scale)
    else:
        z = _odd_mix(mix_refs[0], mix_refs[1], mix_refs[2], mix_par[0], mix_par[1], mix_par[2], pos, own, seq_len)
    x1 = _cat_rows(x_refs) + mod_ref[2:3, :] * _dot(z, wo_ref[...])
    y_ref[...] = x1[HALO:HALO + tm]
    h = _norm_mod(x1, n2_ref[...], mod_ref[3:4, :], mod_ref[4:5, :])
    if own is not None:
        h = h * own
    h_ref[...] = h.astype(BF16)

    for j in range(fdim // fc):
        cols = slice(j * fc, (j + 1) * fc)
        a_ext = _dot(h_ref[...], wu_ref[:, cols])
        dn = pltpu.roll(a_ext, 1, axis=0)[HALO:HALO + tm]
        up = pltpu.roll(a_ext, rows - 1, axis=0)[HALO:HALO + tm]
        if own is None:
            dn = jnp.where(pos[HALO:HALO + tm] == 0, 0.0, dn)
            up = jnp.where(pos[HALO:HALO + tm] == seq_len - 1, 0.0, up)
        a = (dn * cw_ref[0:1, cols] + a_ext[HALO:HALO + tm] * cw_ref[1:2, cols] + up * cw_ref[2:3, cols]
             + cb_ref[:, cols])
        b = _dot(h_ref[HALO:HALO + tm, :], wu_ref[:, fdim + j * fc:fdim + (j + 1) * fc])
        act_ref[:, cols] = (a * _sigmoid(a) * b).astype(BF16)
    y_ref[...] += mod_ref[5:6, :] * _dot(act_ref[...], wd_ref[...])


def _mixer_ffn(kind, x, mix_arrays, mod, w_out, mix_params, norm_g, w_up, conv_w, conv_b, w_down, seq_len, tm, fc,
               post_scale, name):
    t_rows, d = x.shape
    fdim = w_down.shape[0]
    per_seq = mod.shape[0] > 1
    assert tm % HALO == 0 and (seq_len % tm == 0 or tm % seq_len == 0)
    assert seq_len % tm == 0 or not per_seq
    tiles_per_seq = max(seq_len // tm, 1)
    mod_map = (lambda i: (i // tiles_per_seq, 0, 0)) if per_seq else (lambda i: (0, 0, 0))

    def whole(arr, single=False):
        mode = dict(pipeline_mode=pl.Buffered(1)) if single else {}
        return arr, pl.BlockSpec(arr.shape, lambda i: (0,) * arr.ndim, **mode)

    operands = _three(x, tm)
    for arr in mix_arrays:
        operands += _three(arr, tm)
    operands += [(mod, pl.BlockSpec((None, 6, d), mod_map)), whole(norm_g.reshape(1, d)), whole(w_out, True)]
    operands += [whole(a) for a in mix_params]
    operands += [whole(w_up, True), whole(conv_w), whole(conv_b.reshape(1, fdim)), whole(w_down, True)]
    return pl.pallas_call(
        functools.partial(_mixer_ffn_kernel, kind=kind, n_mix=len(mix_arrays), seq_len=seq_len, fc=fc,
                          post_scale=post_scale),
        grid=(t_rows // tm,),
        in_specs=[spec for _, spec in operands],
        out_specs=pl.BlockSpec((tm, d), lambda i: (i, 0)),
        out_shape=jax.ShapeDtypeStruct((t_rows, d), F32),
        scratch_shapes=[pltpu.VMEM((tm + 2 * HALO, d), BF16),
                        pltpu.VMEM((tm, fdim), BF16)],
        compiler_params=_params(("parallel",)),
        name=name,
    )(*[arr for arr, _ in operands])


def _retention_kernel(df_ref, db_ref, q_ref, k_ref, v_ref, kc_ref, vc_ref, o_ref, kv_ref, ss_ref):
    n = q_ref.shape[0]
    lc = kc_ref.shape[0]
    c = RET_CHUNK
    nc = n // c
    pos = lax.broadcasted_iota(jnp.int32, (c, 1), 0).astype(F32)
    ii = lax.broadcasted_iota(jnp.int32, (c, c), 0)
    jj = lax.broadcasted_iota(jnp.int32, (c, c), 1)
    dij = (ii - jj).astype(F32)
    jpos = lax.broadcasted_iota(jnp.int32, (lc, 1), 0).astype(F32)
    lo = _half_mask((1, LANES))
    row_lo = lax.broadcasted_iota(jnp.int32, (LANES, 1), 0) < (LANES // 2)
    block_diag = (lax.broadcasted_iota(jnp.int32, (LANES, 2 * HEAD_V), 0) // (LANES // 2)
                  == lax.broadcasted_iota(jnp.int32, (LANES, 2 * HEAD_V), 1) // HEAD_V)
    zeros_v = jnp.zeros((c, HEAD_V), BF16)

    def cat(a, b):
        return jnp.concatenate([a, b], axis=-1)

    for pair in range(HEADS // 2):
        slab = slice(pair * LANES, (pair + 1) * LANES)
        vcol = slice(pair * 2 * HEAD_V, (pair + 1) * 2 * HEAD_V)
        lg = []
        for d_ref in (df_ref, db_ref):
            lg.append([-jnp.exp(jnp.full((1, 1), d_ref[2 * pair + e], F32)) for e in range(2)])
        (lgf0, lgf1), (lgb0, lgb1) = lg
        lgf_lane = jnp.where(lo, lgf0, lgf1)
        lgb_lane = jnp.where(lo, lgb0, lgb1)
        lgf_row = jnp.where(row_lo, lgf0, lgf1)
        lgb_row = jnp.where(row_lo, lgb0, lgb1)
        qdec = cat(jnp.exp(lgf_lane * (pos + 1.0)), jnp.exp(lgb_lane * (c - pos)))
        kdec = cat(jnp.exp(lgf_lane * (c - 1.0 - pos)), jnp.exp(lgb_lane * pos))
        cdec_f = jnp.exp(lgf_row * float(c))
        cdec_b = jnp.exp(lgb_row * float(c))
        intra = [jnp.where(dij >= 0, jnp.exp(f * jnp.maximum(dij, 0.0)), 0.0)
                 + jnp.where(dij <= 0, jnp.exp(b * jnp.maximum(-dij, 0.0)), 0.0)
                 for f, b in ((lgf0, lgb0), (lgf1, lgb1))]

        for ci in range(nc):
            rows = slice(ci * c, (ci + 1) * c)
            k = k_ref[rows, slab].astype(F32)
            kv_ref[ci] = _dot_tn((cat(k, k) * kdec).astype(BF16), v_ref[rows, vcol].astype(BF16))
        kc = kc_ref[:, slab].astype(F32)
        cdec0 = cat(jnp.exp(lgf_lane * (lc - 1.0 - jpos)), jnp.exp(lgb_lane * jpos))
        s0 = _dot_tn((cat(kc, kc) * cdec0).astype(BF16), vc_ref[:, vcol].astype(BF16))

        s = s0[0:LANES]
        for ci in range(nc):
            ss_ref[ci, 0:LANES, :] = jnp.where(block_diag, s, 0.0).astype(BF16)
            s = s * cdec_f + kv_ref[ci, 0:LANES, :]
        s = s0[LANES:2 * LANES]
        for ci in reversed(range(nc)):
            ss_ref[ci, LANES:2 * LANES, :] = jnp.where(block_diag, s, 0.0).astype(BF16)
            s = s * cdec_b + kv_ref[ci, LANES:2 * LANES, :]

        for ci in range(nc):
            rows = slice(ci * c, (ci + 1) * c)
            q = q_ref[rows, slab].astype(F32)
            kb = k_ref[rows, slab].astype(BF16)
            v = v_ref[rows, vcol].astype(BF16)
            att0 = _dot_nt(jnp.where(lo, q, 0.0).astype(BF16), kb) * intra[0]
            att1 = _dot_nt(jnp.where(lo, 0.0, q).astype(BF16), kb) * intra[1]
            v_blocks = jnp.concatenate([cat(v[:, 0:HEAD_V], zeros_v), cat(zeros_v, v[:, HEAD_V:])], axis=0)
            o_ref[rows, vcol] = (_dot(cat(att0, att1).astype(BF16), v_blocks)
                                 + _dot((cat(q, q) * qdec).astype(BF16), ss_ref[ci])).astype(o_ref.dtype)


def _retention(dec_f, dec_b, q, k, v, kc, vc, seq_len, ctx_len, name):
    t_rows = q.shape[0]
    batch = t_rows // seq_len
    qw, vw = q.shape[1], v.shape[1]
    nc = seq_len // RET_CHUNK
    smem = pl.BlockSpec(memory_space=pltpu.SMEM)
    return pl.pallas_call(
        _retention_kernel,
        grid=(batch,),
        in_specs=[
            smem, smem,
            pl.BlockSpec((seq_len, qw), lambda b: (b, 0)),
            pl.BlockSpec((seq_len, qw), lambda b: (b, 0)),
            pl.BlockSpec((seq_len, vw), lambda b: (b, 0)),
            pl.BlockSpec((ctx_len, qw), lambda b: (b, 0)),
            pl.BlockSpec((ctx_len, vw), lambda b: (b, 0)),
        ],
        out_specs=pl.BlockSpec((seq_len, vw), lambda b: (b, 0)),
        out_shape=jax.ShapeDtypeStruct((t_rows, vw), BF16),
        scratch_shapes=[pltpu.VMEM((nc, 2 * LANES, 2 * HEAD_V), F32),
                        pltpu.VMEM((nc, 2 * LANES, 2 * HEAD_V), BF16)],
        compiler_params=_params(("parallel",)),
        name=name,
    )(dec_f, dec_b, q, k, v, kc, vc)


def _rope_tables(seq_len):
    rows = seq_len // GRID_W
    row = jnp.repeat(jnp.arange(rows, dtype=F32), GRID_W)
    col = jnp.tile(jnp.arange(GRID_W, dtype=F32), rows)
    inv = ROPE_BASE ** (-jnp.arange(ROPE_FREQS, dtype=F32) / ROPE_FREQS)
    ar, ac = row[:, None] * inv, col[:, None] * inv
    cos = jnp.concatenate([jnp.cos(ar), jnp.cos(ar), jnp.cos(ac), jnp.cos(ac)], axis=-1)
    sin = jnp.concatenate([-jnp.sin(ar), jnp.sin(ar), -jnp.sin(ac), jnp.sin(ac)], axis=-1)
    reps = LANES // ROPE_HEAD_DIM
    return jnp.tile(cos, (1, reps)), jnp.tile(sin, (1, reps))


def _tile(total, want):
    t = min(want, total)
    assert total % t == 0
    return t


def kernel(x, c, ctx, c_ctx, mod_w, mod_b, norm1_g, norm2_g, ev_w_in, ev_w_out, sc_conv_w, da_q_norm, da_k_norm,
           da_lq1, da_lk1, da_lq2, da_lk2, da_subln_g, od_w_in, od_w_out, pool_w, pool_scale, ret_decay_f,
           ret_decay_b, ret_gn_g, ffn_w_up, ffn_conv_w, ffn_conv_b, ffn_w_down):
    batch, seq, d = x.shape
    ctx_len = ctx.shape[1]
    depth = mod_w.shape[0]
    t_lat, t_ctx = batch * seq, batch * ctx_len
    xs = x.reshape(t_lat, d)
    cs = ctx.reshape(t_ctx, d)

    pad = (-(batch + 1)) % 8
    c_all = jnp.concatenate([c, c_ctx[None, :], jnp.zeros((pad, d), F32)], axis=0)
    mods = _modulation(c_all, mod_w, mod_b)
    rope = _rope_tables(seq)

    tm_lat = _tile(seq, 512)
    tm_ctx = _tile(t_ctx, 512)
    tm_ffn_lat = _tile(seq, 1024)
    tm_ffn_ctx = _tile(t_ctx, 512)
    tq = _tile(seq, ATTN_ROWS)
    fdim = ffn_w_down.shape[1]
    fc = 256 if fdim % 256 == 0 else LANES

    for i in range(depth):
        last = i == depth - 1
        j = i // 2
        mod_l = mods[i, :batch].reshape(batch, 6, d)
        mod_c = mods[i, batch:batch + 1].reshape(1, 6, d)
        ffn_w = (ffn_w_up[i].astype(BF16), ffn_conv_w[i], ffn_conv_b[i], ffn_w_down[i].astype(BF16))
        if i % 2 == 0:
            w_in = ev_w_in[j].astype(BF16)
            w_out = ev_w_out[j].astype(BF16)
            scw = ev_w_out.shape[1] // 2
            qkw = (w_in.shape[1] - 4 * scw) // 2
            q_scale = ROPE_HEAD_DIM ** -0.5 * LOG2E
            segs = [(3 * scw, 3 * scw + qkw, "qknorm", 0, q_scale),
                    (3 * scw + qkw, 3 * scw + 2 * qkw, "qknorm", 1, 1.0),
                    (0, 3 * scw, "copy", 0, 1.0),
                    (3 * scw + 2 * qkw, w_in.shape[1], "copy", 0, 1.0)]
            dts = [BF16, BF16, BF16, BF16]
            gains = [da_q_norm[j], da_k_norm[j]]
            lam_init = 0.8 - 0.6 * math.exp(-0.3 * i)
            lam_vecs = jnp.stack([da_lq1[j], da_lk1[j], da_lq2[j], da_lk2[j]], axis=0)
            bound = (ROPE_HEAD_DIM * q_scale * BOUND_MARGIN * jnp.max(jnp.abs(da_q_norm[j]))
                     * jnp.max(jnp.abs(da_k_norm[j]))).reshape(1).astype(F32)
            q_l, k_l, g_l, v_l = _in_proj(xs, mod_l, norm1_g[i], w_in, segs, dts, gains, rope, seq, tm_lat,
                                          "in_even_lat")
            q_c, k_c, g_c, v_c = _in_proj(cs, mod_c, norm1_g[i], w_in, segs, dts, gains, None, ctx_len, tm_ctx,
                                          "in_even_ctx")
            o_l = _diff_attn(bound, lam_vecs, q_l, [(k_l, v_l, seq), (k_c, v_c, ctx_len)], seq, tq, 1, lam_init,
                             "attn_lat")
            mix_params = [sc_conv_w[j], da_subln_g[j].reshape(1, HEAD_V)]
            xs = _mixer_ffn("even", xs, [g_l, o_l], mod_l, w_out, mix_params, norm2_g[i], *ffn_w, seq, tm_ffn_lat, fc,
                            1.0 - lam_init, "mix_ffn_even_lat")
            if not last:
                o_c = _diff_attn(bound, lam_vecs, q_c, [(k_c, v_c, ctx_len)], ctx_len, ctx_len, HEADS, lam_init,
                                 "attn_ctx")
                cs = _mixer_ffn("even", cs, [g_c, o_c], mod_c, w_out, mix_params, norm2_g[i], *ffn_w, ctx_len,
                                tm_ffn_ctx, fc, 1.0 - lam_init, "mix_ffn_even_ctx")
        else:
            assert last, "an odd layer that still updates the context stream is not implemented"
            w_in = od_w_in[j].astype(BF16)
            w_out = od_w_out[j].astype(BF16)
            pw = pool_scale.shape[1]
            vw = HEADS * HEAD_V
            qw = (w_in.shape[1] - pw - 2 * vw) // 2
            segs = [(pw, pw + qw, "rope", 0, ROPE_HEAD_DIM ** -0.5),
                    (pw + qw, pw + 2 * qw, "rope", 0, 1.0),
                    (0, pw, "copy", 0, 1.0),
                    (pw + 2 * qw, pw + 2 * qw + vw, "copy", 0, 1.0),
                    (pw + 2 * qw + vw, w_in.shape[1], "copy", 0, 1.0)]
            q_l, k_l, pv, v_l, gate = _in_proj(xs, mod_l, norm1_g[i], w_in, segs, [BF16] * 5, [], rope, seq, tm_lat,
                                               "in_odd_lat")
            w_kv = w_in[:, pw + qw:pw + 2 * qw + vw]
            k_c, v_c = _in_proj(cs, mod_c, norm1_g[i], w_kv, [(0, qw, "copy", 0, 1.0), (qw, qw + vw, "copy", 0, 1.0)],
                                [BF16, BF16], [], None, ctx_len, tm_ctx, "in_odd_ctx")
            o_l = _retention(ret_decay_f[j], ret_decay_b[j], q_l, k_l, v_l, k_c, v_c, seq, ctx_len, "retention")
            mix_params = [pool_w[j].astype(BF16), pool_scale[j].reshape(1, pw), ret_gn_g[j].reshape(1, HEAD_V)]
            xs = _mixer_ffn("odd", xs, [pv, gate, o_l], mod_l, w_out, mix_params, norm2_g[i], *ffn_w, seq, tm_ffn_lat,
                            fc, 1.0, "mix_ffn_odd_lat")
    return xs.reshape(batch, seq, d)
```

```python
import functools
import math

import jax
import jax.numpy as jnp
from jax import lax
from jax.experimental import pallas as pl
from jax.experimental.pallas import tpu as pltpu

F32 = jnp.float32
BF16 = jnp.bfloat16

EPS = 1e-6
GRID_W = 64
ROPE_HEAD_DIM = 64
ROPE_FREQS = ROPE_HEAD_DIM // 4
ROPE_BASE = 10000.0
HEADS = 4
HEAD_V = 128
RET_CHUNK = 128
POOL_RADII = (1, 2, 4, 8)
LANES = 128
HALO = 16
VMEM_LIMIT = 56 * 1024 * 1024
LOG2E = math.log2(math.e)
BOUND_MARGIN = 1.02
MAX_SAFE_SCORE_BOUND = 40.0
ATTN_ROWS = 2048


def _params(sem, vmem=VMEM_LIMIT):
    return pltpu.CompilerParams(dimension_semantics=sem, vmem_limit_bytes=vmem)


def _sigmoid(x):
    return 1.0 / (1.0 + jnp.exp(-x))


def _dot(a, b):
    return jnp.dot(a, b, preferred_element_type=F32)


def _dot_nt(a, b):
    return lax.dot_general(a, b, (((1,), (1,)), ((), ())), preferred_element_type=F32)


def _dot_tn(a, b):
    return lax.dot_general(a, b, (((0,), (0,)), ((), ())), preferred_element_type=F32)


def _rms(x, width):
    return x * lax.rsqrt(jnp.sum(x * x, axis=-1, keepdims=True) * (1.0 / width) + EPS)


def _norm_mod(x, g, shift, scale):
    return (_rms(x, x.shape[-1]) * g) * (1.0 + scale) + shift


def _half_mask(shape):
    return lax.broadcasted_iota(jnp.int32, shape, len(shape) - 1) < (LANES // 2)


def _rope(t, cos, sin):
    lane = lax.broadcasted_iota(jnp.int32, t.shape, 1)
    first = lax.rem(lane, 2 * ROPE_FREQS) < ROPE_FREQS
    partner = jnp.where(first, pltpu.roll(t, LANES - ROPE_FREQS, axis=1), pltpu.roll(t, ROPE_FREQS, axis=1))
    return t * cos + partner * sin


def _norm64(t, gain):
    lo = _half_mask(t.shape)
    sq = t * t
    s_lo = jnp.sum(jnp.where(lo, sq, 0.0), axis=-1, keepdims=True)
    s_hi = jnp.sum(jnp.where(lo, 0.0, sq), axis=-1, keepdims=True)
    w = 1.0 / (LANES // 2)
    r = jnp.where(lo, lax.rsqrt(s_lo * w + EPS), lax.rsqrt(s_hi * w + EPS))
    return (t * r) * gain


def _mod_kernel(c_ref, w_ref, b_ref, o_ref):
    c = c_ref[...]
    a = c * _sigmoid(c)
    o_ref[...] = _dot(a.astype(BF16), w_ref[...].astype(BF16)) + b_ref[...]


def _modulation(c_all, mod_w, mod_b):
    depth, d, n6 = mod_w.shape
    rows = c_all.shape[0]
    tn = 1536
    return pl.pallas_call(
        _mod_kernel,
        grid=(depth, n6 // tn),
        in_specs=[
            pl.BlockSpec((rows, d), lambda l, j: (0, 0)),
            pl.BlockSpec((None, d, tn), lambda l, j: (l, 0, j)),
            pl.BlockSpec((None, 1, tn), lambda l, j: (l, 0, j)),
        ],
        out_specs=pl.BlockSpec((None, rows, tn), lambda l, j: (l, 0, j)),
        out_shape=jax.ShapeDtypeStruct((depth, rows, n6), F32),
        compiler_params=_params(("parallel", "parallel")),
        name="modulation",
    )(c_all, mod_w, mod_b.reshape(depth, 1, n6))


def _in_proj_kernel(*refs, segs, rotate, n_gain):
    x_ref, mod_ref, g_ref, w_ref = refs[:4]
    gain_refs = refs[4:4 + n_gain]
    pos = 4 + n_gain
    if rotate:
        cos_ref, sin_ref = refs[pos:pos + 2]
        pos += 2
    out_refs = refs[pos:]
    h = _norm_mod(x_ref[...], g_ref[...], mod_ref[0:1, :], mod_ref[1:2, :]).astype(BF16)
    for (c0, c1, mode, gain_idx, scale), o_ref in zip(segs, out_refs):
        y = _dot(h, w_ref[:, c0:c1])
        if mode == "copy":
            o_ref[...] = y.astype(o_ref.dtype)
            continue
        for s in range((c1 - c0) // LANES):
            t = y[:, s * LANES:(s + 1) * LANES]
            if mode == "qknorm":
                t = _norm64(t, gain_refs[gain_idx][...])
            if rotate:
                t = _rope(t, cos_ref[...], sin_ref[...])
            if scale != 1.0:
                t = t * scale
            o_ref[:, s * LANES:(s + 1) * LANES] = t.astype(o_ref.dtype)


def _in_proj(x, mod, g, w, segs, out_dtypes, gains, rope, seq_len, tm, name):
    t_rows, d = x.shape
    per_seq = mod.shape[0] > 1
    tiles_per_seq = max(seq_len // tm, 1)
    if per_seq:
        assert seq_len % tm == 0
        mod_map = lambda i: (i // tiles_per_seq, 0, 0)
    else:
        mod_map = lambda i: (0, 0, 0)
    in_specs = [
        pl.BlockSpec((tm, d), lambda i: (i, 0)),
        pl.BlockSpec((None, 6, d), mod_map),
        pl.BlockSpec((1, d), lambda i: (0, 0)),
        pl.BlockSpec(w.shape, lambda i: (0, 0)),
    ]
    args = [x, mod, g.reshape(1, d), w]
    for gn in gains:
        in_specs.append(pl.BlockSpec((1, LANES), lambda i: (0, 0)))
        args.append(jnp.tile(gn.reshape(1, -1), (1, LANES // gn.shape[-1])))
    if rope is not None:
        assert seq_len % tm == 0
        for tab in rope:
            in_specs.append(pl.BlockSpec((tm, LANES), lambda i: (i % tiles_per_seq, 0)))
            args.append(tab)
    out_specs = [pl.BlockSpec((tm, c1 - c0), lambda i: (i, 0)) for (c0, c1, _, _, _) in segs]
    out_shape = [jax.ShapeDtypeStruct((t_rows, c1 - c0), dt) for (c0, c1, _, _, _), dt in zip(segs, out_dtypes)]
    return pl.pallas_call(
        functools.partial(_in_proj_kernel, segs=tuple(segs), rotate=rope is not None, n_gain=len(gains)),
        grid=(t_rows // tm,),
        in_specs=in_specs,
        out_specs=out_specs,
        out_shape=out_shape,
        compiler_params=_params(("parallel",)),
        name=name,
    )(*args)


def _diff_attn_kernel(*refs, n_src, lam_init):
    bound_ref, lam_ref, q_ref = refs[:3]
    kv_refs = refs[3:3 + 2 * n_src]
    o_ref = refs[3 + 2 * n_src]
    lv = lam_ref[...]
    lam = (jnp.exp(jnp.sum(lv[0:1] * lv[1:2], axis=-1, keepdims=True))
           - jnp.exp(jnp.sum(lv[2:3] * lv[3:4], axis=-1, keepdims=True)) + lam_init)
    bound = bound_ref[0]

    def attend_head(cols, use_bound):
        q = q_ref[:, cols]
        ks = [kv_refs[2 * j][:, cols] for j in range(n_src)]
        vs = [kv_refs[2 * j + 1][:, cols] for j in range(n_src)]

        def softmax_av(first_map):
            ss = []
            for k in ks:
                lo = _half_mask(k.shape)
                zero = jnp.zeros_like(k)
                ss.append(_dot_nt(jnp.where(lo, k, zero) if first_map else jnp.where(lo, zero, k), q))
            if use_bound:
                m = bound
            else:
                m = functools.reduce(jnp.maximum, [jnp.max(s, axis=0, keepdims=True) for s in ss])
            ps = [jnp.exp2(s - m) for s in ss]
            l = functools.reduce(jnp.add, [jnp.sum(p, axis=0, keepdims=True) for p in ps])
            acc = functools.reduce(jnp.add, [_dot_tn(v, p.astype(BF16)) for v, p in zip(vs, ps)])
            return acc, l

        acc0, l0 = softmax_av(True)
        acc1, l1 = softmax_av(False)
        o_t = acc0 * (1.0 / l0) - acc1 * (lam / l1)
        o_ref[:, cols] = o_t.T.astype(o_ref.dtype)

    def attend(use_bound):
        for hd in range(q_ref.shape[1] // HEAD_V):
            attend_head(slice(hd * HEAD_V, (hd + 1) * HEAD_V), use_bound)

    safe = bound <= MAX_SAFE_SCORE_BOUND

    @pl.when(safe)
    def _():
        attend(True)

    @pl.when(jnp.logical_not(safe))
    def _():
        attend(False)


def _diff_attn(bound, lam_vecs, q, kvs, seq_q, tq, heads_per_step, lam_init, name):
    t_rows, width = q.shape
    batch = t_rows // seq_q
    nq = seq_q // tq
    bw = heads_per_step * HEAD_V
    in_specs = [
        pl.BlockSpec(memory_space=pltpu.SMEM),
        pl.BlockSpec(lam_vecs.shape, lambda b, h, i: (0, 0)),
        pl.BlockSpec((tq, bw), lambda b, h, i: (b * nq + i, h)),
    ]
    args = [bound, lam_vecs, q]
    for k, v, seq_k in kvs:
        in_specs.append(pl.BlockSpec((seq_k, bw), lambda b, h, i: (b, h)))
        in_specs.append(pl.BlockSpec((seq_k, bw), lambda b, h, i: (b, h)))
        args += [k, v]
    return pl.pallas_call(
        functools.partial(_diff_attn_kernel, n_src=len(kvs), lam_init=lam_init),
        grid=(batch, width // bw, nq),
        in_specs=in_specs,
        out_specs=pl.BlockSpec((tq, bw), lambda b, h, i: (b * nq + i, h)),
        out_shape=jax.ShapeDtypeStruct((t_rows, width), BF16),
        compiler_params=_params(("parallel", "parallel", "parallel")),
        name=name,
    )(*args)


def _three(arr, tm):
    t_rows, width = arr.shape
    per = tm // HALO
    last = t_rows // HALO - 1
    prev = pl.BlockSpec((HALO, width), lambda i: (jnp.maximum(i * per - 1, 0), 0))
    nxt = pl.BlockSpec((HALO, width), lambda i: (jnp.minimum((i + 1) * per, last), 0))
    return [(arr, prev), (arr, pl.BlockSpec((tm, width), lambda i: (i, 0))), (arr, nxt)]


def _cat_rows(refs):
    return jnp.concatenate([r[...] for r in refs], axis=0)


def _tile_rows(tm, seq_len):
    row = lax.broadcasted_iota(jnp.int32, (tm + 2 * HALO, 1), 0)
    if tm > seq_len:
        return lax.rem(pl.program_id(0) * tm + (seq_len - HALO) + row, seq_len), None
    pos = lax.rem(pl.program_id(0) * tm, seq_len) - HALO + row
    return pos, jnp.logical_and(pos >= 0, pos < seq_len).astype(F32)


def _shift_rows(u, pos, own, seq_len):
    rows = u.shape[0]
    dn = pltpu.roll(u, 1, axis=0)
    up = pltpu.roll(u, rows - 1, axis=0)
    if own is None:
        dn = jnp.where(pos == 0, 0.0, dn)
        up = jnp.where(pos == seq_len - 1, 0.0, up)
    return dn, up


def _head_rms(o, gain, post_scale=1.0):
    parts = []
    for h in range(o.shape[-1] // HEAD_V):
        t = _rms(o[:, h * HEAD_V:(h + 1) * HEAD_V].astype(F32), HEAD_V) * gain
        parts.append(t * post_scale if post_scale != 1.0 else t)
    return parts


def _even_mix(gate_refs, o_refs, cw_ref, sg_ref, pos, own, seq_len, post_scale):
    gates = _cat_rows(gate_refs).astype(F32)
    w = gates.shape[1] // 3
    u = gates[:, w:2 * w] * gates[:, 2 * w:3 * w]
    if own is not None:
        u = u * own
    dn, up = _shift_rows(u, pos, own, seq_len)
    y_conv = gates[:, 0:w] * (dn * cw_ref[0:1, :] + u * cw_ref[1:2, :] + up * cw_ref[2:3, :])
    heads = _head_rms(_cat_rows(o_refs), sg_ref[...], post_scale)
    return jnp.concatenate([y_conv.astype(BF16)] + [p.astype(BF16) for p in heads], axis=-1)


def _odd_mix(pv_refs, gate_refs, o_refs, pw_ref, ps_ref, gg_ref, pos, own, seq_len):
    assert own is not None and max(POOL_RADII) < HALO
    pv = _cat_rows(pv_refs).astype(F32) * own
    rows = pv.shape[0]
    parts = []
    for gi, r in enumerate(POOL_RADII):
        v = pv[:, gi * LANES:(gi + 1) * LANES]
        win = v
        span = 1
        while span < 2 * r:
            win = win + pltpu.roll(win, rows - span, axis=0)
            span *= 2
        total = pltpu.roll(win, r, axis=0) + pltpu.roll(v, rows - r, axis=0)
        cnt = jnp.maximum(jnp.minimum(pos + r + 1, seq_len) - jnp.maximum(pos - r, 0), 1).astype(F32)
        y = (total / cnt - v).astype(BF16)
        parts.append(_dot(y, pw_ref[gi]))
    y_p = jnp.concatenate(parts, axis=-1) * ps_ref[...]
    gate = _cat_rows(gate_refs).astype(F32)
    y_r = jnp.concatenate(_head_rms(_cat_rows(o_refs), gg_ref[...]), axis=-1) * (gate * _sigmoid(gate))
    return jnp.concatenate([y_p.astype(BF16), y_r.astype(BF16)], axis=-1)


def _mixer_ffn_kernel(*refs, kind, n_mix, seq_len, fc, post_scale):
    x_refs = refs[0:3]
    mix_refs = [refs[3 + 3 * k:6 + 3 * k] for k in range(n_mix)]
    p = 3 + 3 * n_mix
    mod_ref, n2_ref, wo_ref = refs[p:p + 3]
    n_par = 2 if kind == "even" else 3
    mix_par = refs[p + 3:p + 3 + n_par]
    wu_ref, cw_ref, cb_ref, wd_ref, y_ref, h_ref, act_ref = refs[p + 3 + n_par:]
    tm = y_ref.shape[0]
    rows = tm + 2 * HALO
    fdim = wd_ref.shape[0]
    pos, own = _tile_rows(tm, seq_len)

    if kind == "even":
        z = _even_mix(mix_refs[0], mix_refs[1], mix_par[0], mix_par[1], pos, own, seq_len, post_scale)
    else:
        z = _odd_mix(mix_refs[0], mix_refs[1], mix_refs[2], mix_par[0], mix_par[1], mix_par[2], pos, own, seq_len)
    x1 = _cat_rows(x_refs) + mod_ref[2:3, :] * _dot(z, wo_ref[...])
    y_ref[...] = x1[HALO:HALO + tm]
    h = _norm_mod(x1, n2_ref[...], mod_ref[3:4, :], mod_ref[4:5, :])
    if own is not None:
        h = h * own
    h_ref[...] = h.astype(BF16)

    for j in range(fdim // fc):
        cols = slice(j * fc, (j + 1) * fc)
        a_ext = _dot(h_ref[...], wu_ref[:, cols])
        dn = pltpu.roll(a_ext, 1, axis=0)[HALO:HALO + tm]
        up = pltpu.roll(a_ext, rows - 1, axis=0)[HALO:HALO + tm]
        if own is None:
            dn = jnp.where(pos[HALO:HALO + tm] == 0, 0.0, dn)
            up = jnp.where(pos[HALO:HALO + tm] == seq_len - 1, 0.0, up)
        a = (dn * cw_ref[0:1, cols] + a_ext[HALO:HALO + tm] * cw_ref[1:2, cols] + up * cw_ref[2:3, cols]
             + cb_ref[:, cols])
        b = _dot(h_ref[HALO:HALO + tm, :], wu_ref[:, fdim + j * fc:fdim + (j + 1) * fc])
        act_ref[:, cols] = (a * _sigmoid(a) * b).astype(BF16)
    y_ref[...] += mod_ref[5:6, :] * _dot(act_ref[...], wd_ref[...])


def _mixer_ffn(kind, x, mix_arrays, mod, w_out, mix_params, norm_g, w_up, conv_w, conv_b, w_down, seq_len, tm, fc,
               post_scale, name):
    t_rows, d = x.shape
    fdim = w_down.shape[0]
    per_seq = mod.shape[0] > 1
    assert tm % HALO == 0 and (seq_len % tm == 0 or tm % seq_len == 0)
    assert seq_len % tm == 0 or not per_seq
    tiles_per_seq = max(seq_len // tm, 1)
    mod_map = (lambda i: (i // tiles_per_seq, 0, 0)) if per_seq else (lambda i: (0, 0, 0))

    def whole(arr, single=False):
        mode = dict(pipeline_mode=pl.Buffered(1)) if single else {}
        return arr, pl.BlockSpec(arr.shape, lambda i: (0,) * arr.ndim, **mode)

    operands = _three(x, tm)
    for arr in mix_arrays:
        operands += _three(arr, tm)
    operands += [(mod, pl.BlockSpec((None, 6, d), mod_map)), whole(norm_g.reshape(1, d)), whole(w_out, True)]
    operands += [whole(a) for a in mix_params]
    operands += [whole(w_up, True), whole(conv_w), whole(conv_b.reshape(1, fdim)), whole(w_down, True)]
    return pl.pallas_call(
        functools.partial(_mixer_ffn_kernel, kind=kind, n_mix=len(mix_arrays), seq_len=seq_len, fc=fc,
                          post_scale=post_scale),
        grid=(t_rows // tm,),
        in_specs=[spec for _, spec in operands],
        out_specs=pl.BlockSpec((tm, d), lambda i: (i, 0)),
        out_shape=jax.ShapeDtypeStruct((t_rows, d), F32),
        scratch_shapes=[pltpu.VMEM((tm + 2 * HALO, d), BF16),
                        pltpu.VMEM((tm, fdim), BF16)],
        compiler_params=_params(("parallel",)),
        name=name,
    )(*[arr for arr, _ in operands])


def _retention_kernel(df_ref, db_ref, q_ref, k_ref, v_ref, kc_ref, vc_ref, o_ref, kv_ref, ss_ref):
    n = q_ref.shape[0]
    lc = kc_ref.shape[0]
    c = RET_CHUNK
    nc = n // c
    pos = lax.broadcasted_iota(jnp.int32, (c, 1), 0).astype(F32)
    ii = lax.broadcasted_iota(jnp.int32, (c, c), 0)
    jj = lax.broadcasted_iota(jnp.int32, (c, c), 1)
    dij = (ii - jj).astype(F32)
    jpos = lax.broadcasted_iota(jnp.int32, (lc, 1), 0).astype(F32)
    lo = _half_mask((1, LANES))
    row_lo = lax.broadcasted_iota(jnp.int32, (LANES, 1), 0) < (LANES // 2)
    block_diag = (lax.broadcasted_iota(jnp.int32, (LANES, 2 * HEAD_V), 0) // (LANES // 2)
                  == lax.broadcasted_iota(jnp.int32, (LANES, 2 * HEAD_V), 1) // HEAD_V)
    zeros_v = jnp.zeros((c, HEAD_V), BF16)

    def cat(a, b):
        return jnp.concatenate([a, b], axis=-1)

    for pair in range(HEADS // 2):
        slab = slice(pair * LANES, (pair + 1) * LANES)
        vcol = slice(pair * 2 * HEAD_V, (pair + 1) * 2 * HEAD_V)
        lg = []
        for d_ref in (df_ref, db_ref):
            lg.append([-jnp.exp(jnp.full((1, 1), d_ref[2 * pair + e], F32)) for e in range(2)])
        (lgf0, lgf1), (lgb0, lgb1) = lg
        lgf_lane = jnp.where(lo, lgf0, lgf1)
        lgb_lane = jnp.where(lo, lgb0, lgb1)
        lgf_row = jnp.where(row_lo, lgf0, lgf1)
        lgb_row = jnp.where(row_lo, lgb0, lgb1)
        qdec = cat(jnp.exp(lgf_lane * (pos + 1.0)), jnp.exp(lgb_lane * (c - pos)))
        kdec = cat(jnp.exp(lgf_lane * (c - 1.0 - pos)), jnp.exp(lgb_lane * pos))
        cdec_f = jnp.exp(lgf_row * float(c))
        cdec_b = jnp.exp(lgb_row * float(c))
        intra = [jnp.where(dij >= 0, jnp.exp(f * jnp.maximum(dij, 0.0)), 0.0)
                 + jnp.where(dij <= 0, jnp.exp(b * jnp.maximum(-dij, 0.0)), 0.0)
                 for f, b in ((lgf0, lgb0), (lgf1, lgb1))]

        for ci in range(nc):
            rows = slice(ci * c, (ci + 1) * c)
            k = k_ref[rows, slab].astype(F32)
            kv_ref[ci] = _dot_tn((cat(k, k) * kdec).astype(BF16), v_ref[rows, vcol].astype(BF16))
        kc = kc_ref[:, slab].astype(F32)
        cdec0 = cat(jnp.exp(lgf_lane * (lc - 1.0 - jpos)), jnp.exp(lgb_lane * jpos))
        s0 = _dot_tn((cat(kc, kc) * cdec0).astype(BF16), vc_ref[:, vcol].astype(BF16))

        s = s0[0:LANES]
        for ci in range(nc):
            ss_ref[ci, 0:LANES, :] = jnp.where(block_diag, s, 0.0).astype(BF16)
            s = s * cdec_f + kv_ref[ci, 0:LANES, :]
        s = s0[LANES:2 * LANES]
        for ci in reversed(range(nc)):
            ss_ref[ci, LANES:2 * LANES, :] = jnp.where(block_diag, s, 0.0).astype(BF16)
            s = s * cdec_b + kv_ref[ci, LANES:2 * LANES, :]

        for ci in range(nc):
            rows = slice(ci * c, (ci + 1) * c)
            q = q_ref[rows, slab].astype(F32)
            kb = k_ref[rows, slab].astype(BF16)
            v = v_ref[rows, vcol].astype(BF16)
            att0 = _dot_nt(jnp.where(lo, q, 0.0).astype(BF16), kb) * intra[0]
            att1 = _dot_nt(jnp.where(lo, 0.0, q).astype(BF16), kb) * intra[1]
            v_blocks = jnp.concatenate([cat(v[:, 0:HEAD_V], zeros_v), cat(zeros_v, v[:, HEAD_V:])], axis=0)
            o_ref[rows, vcol] = (_dot(cat(att0, att1).astype(BF16), v_blocks)
                                 + _dot((cat(q, q) * qdec).astype(BF16), ss_ref[ci])).astype(o_ref.dtype)


def _retention(dec_f, dec_b, q, k, v, kc, vc, seq_len, ctx_len, name):
    t_rows = q.shape[0]
    batch = t_rows // seq_len
    qw, vw = q.shape[1], v.shape[1]
    nc = seq_len // RET_CHUNK
    smem = pl.BlockSpec(memory_space=pltpu.SMEM)
    return pl.pallas_call(
        _retention_kernel,
        grid=(batch,),
        in_specs=[
            smem, smem,
            pl.BlockSpec((seq_len, qw), lambda b: (b, 0)),
            pl.BlockSpec((seq_len, qw), lambda b: (b, 0)),
            pl.BlockSpec((seq_len, vw), lambda b: (b, 0)),
            pl.BlockSpec((ctx_len, qw), lambda b: (b, 0)),
            pl.BlockSpec((ctx_len, vw), lambda b: (b, 0)),
        ],
        out_specs=pl.BlockSpec((seq_len, vw), lambda b: (b, 0)),
        out_shape=jax.ShapeDtypeStruct((t_rows, vw), BF16),
        scratch_shapes=[pltpu.VMEM((nc, 2 * LANES, 2 * HEAD_V), F32),
                        pltpu.VMEM((nc, 2 * LANES, 2 * HEAD_V), BF16)],
        compiler_params=_params(("parallel",)),
        name=name,
    )(dec_f, dec_b, q, k, v, kc, vc)


def _rope_tables(seq_len):
    rows = seq_len // GRID_W
    row = jnp.repeat(jnp.arange(rows, dtype=F32), GRID_W)
    col = jnp.tile(jnp.arange(GRID_W, dtype=F32), rows)
    inv = ROPE_BASE ** (-jnp.arange(ROPE_FREQS, dtype=F32) / ROPE_FREQS)
    ar, ac = row[:, None] * inv, col[:, None] * inv
    cos = jnp.concatenate([jnp.cos(ar), jnp.cos(ar), jnp.cos(ac), jnp.cos(ac)], axis=-1)
    sin = jnp.concatenate([-jnp.sin(ar), jnp.sin(ar), -jnp.sin(ac), jnp.sin(ac)], axis=-1)
    reps = LANES // ROPE_HEAD_DIM
    return jnp.tile(cos, (1, reps)), jnp.tile(sin, (1, reps))


def _tile(total, want):
    t = min(want, total)
    assert total % t == 0
    return t


def kernel(x, c, ctx, c_ctx, mod_w, mod_b, norm1_g, norm2_g, ev_w_in, ev_w_out, sc_conv_w, da_q_norm, da_k_norm,
           da_lq1, da_lk1, da_lq2, da_lk2, da_subln_g, od_w_in, od_w_out, pool_w, pool_scale, ret_decay_f,
           ret_decay_b, ret_gn_g, ffn_w_up, ffn_conv_w, ffn_conv_b, ffn_w_down):
    batch, seq, d = x.shape
    ctx_len = ctx.shape[1]
    depth = mod_w.shape[0]
    t_lat, t_ctx = batch * seq, batch * ctx_len
    xs = x.reshape(t_lat, d)
    cs = ctx.reshape(t_ctx, d)

    pad = (-(batch + 1)) % 8
    c_all = jnp.concatenate([c, c_ctx[None, :], jnp.zeros((pad, d), F32)], axis=0)
    mods = _modulation(c_all, mod_w, mod_b)
    rope = _rope_tables(seq)

    tm_lat = _tile(seq, 1024)
    tm_ctx = _tile(t_ctx, 512)
    tm_ffn_lat = _tile(seq, 1024)
    tm_ffn_ctx = _tile(t_ctx, 512)
    tq = _tile(seq, ATTN_ROWS)
    fdim = ffn_w_down.shape[1]
    fc = 256 if fdim % 256 == 0 else LANES
    ffn_up, ffn_down = ffn_w_up.astype(BF16), ffn_w_down.astype(BF16)

    for i in range(depth):
        last = i == depth - 1
        j = i // 2
        mod_l = mods[i, :batch].reshape(batch, 6, d)
        mod_c = mods[i, batch:batch + 1].reshape(1, 6, d)
        ffn_w = (ffn_up[i], ffn_conv_w[i], ffn_conv_b[i], ffn_down[i])
        if i % 2 == 0:
            w_in = ev_w_in[j].astype(BF16)
            w_out = ev_w_out[j].astype(BF16)
            scw = ev_w_out.shape[1] // 2
            qkw = (w_in.shape[1] - 4 * scw) // 2
            q_scale = ROPE_HEAD_DIM ** -0.5 * LOG2E
            segs = [(3 * scw, 3 * scw + qkw, "qknorm", 0, q_scale),
                    (3 * scw + qkw, 3 * scw + 2 * qkw, "qknorm", 1, 1.0),
                    (0, 3 * scw, "copy", 0, 1.0),
                    (3 * scw + 2 * qkw, w_in.shape[1], "copy", 0, 1.0)]
            dts = [BF16, BF16, BF16, BF16]
            gains = [da_q_norm[j], da_k_norm[j]]
            lam_init = 0.8 - 0.6 * math.exp(-0.3 * i)
            lam_vecs = jnp.stack([da_lq1[j], da_lk1[j], da_lq2[j], da_lk2[j]], axis=0)
            bound = (ROPE_HEAD_DIM * q_scale * BOUND_MARGIN * jnp.max(jnp.abs(da_q_norm[j]))
                     * jnp.max(jnp.abs(da_k_norm[j]))).reshape(1).astype(F32)
            q_l, k_l, g_l, v_l = _in_proj(xs, mod_l, norm1_g[i], w_in, segs, dts, gains, rope, seq, tm_lat,
                                          "in_even_lat")
            q_c, k_c, g_c, v_c = _in_proj(cs, mod_c, norm1_g[i], w_in, segs, dts, gains, None, ctx_len, tm_ctx,
                                          "in_even_ctx")
            o_l = _diff_attn(bound, lam_vecs, q_l, [(k_l, v_l, seq), (k_c, v_c, ctx_len)], seq, tq, 1, lam_init,
                             "attn_lat")
            mix_params = [sc_conv_w[j], da_subln_g[j].reshape(1, HEAD_V)]
            xs = _mixer_ffn("even", xs, [g_l, o_l], mod_l, w_out, mix_params, norm2_g[i], *ffn_w, seq, tm_ffn_lat, fc,
                            1.0 - lam_init, "mix_ffn_even_lat")
            if not last:
                o_c = _diff_attn(bound, lam_vecs, q_c, [(k_c, v_c, ctx_len)], ctx_len, ctx_len, HEADS, lam_init,
                                 "attn_ctx")
                cs = _mixer_ffn("even", cs, [g_c, o_c], mod_c, w_out, mix_params, norm2_g[i], *ffn_w, ctx_len,
                                tm_ffn_ctx, fc, 1.0 - lam_init, "mix_ffn_even_ctx")
        else:
            assert last, "an odd layer that still updates the context stream is not implemented"
            w_in = od_w_in[j].astype(BF16)
            w_out = od_w_out[j].astype(BF16)
            pw = pool_scale.shape[1]
            vw = HEADS * HEAD_V
            qw = (w_in.shape[1] - pw - 2 * vw) // 2
            segs = [(pw, pw + qw, "rope", 0, ROPE_HEAD_DIM ** -0.5),
                    (pw + qw, pw + 2 * qw, "rope", 0, 1.0),
                    (0, pw, "copy", 0, 1.0),
                    (pw + 2 * qw, pw + 2 * qw + vw, "copy", 0, 1.0),
                    (pw + 2 * qw + vw, w_in.shape[1], "copy", 0, 1.0)]
            q_l, k_l, pv, v_l, gate = _in_proj(xs, mod_l, norm1_g[i], w_in, segs, [BF16] * 5, [], rope, seq, tm_lat,
                                               "in_odd_lat")
            w_kv = w_in[:, pw + qw:pw + 2 * qw + vw]
            k_c, v_c = _in_proj(cs, mod_c, norm1_g[i], w_kv, [(0, qw, "copy", 0, 1.0), (qw, qw + vw, "copy", 0, 1.0)],
                                [BF16, BF16], [], None, ctx_len, tm_ctx, "in_odd_ctx")
            o_l = _retention(ret_decay_f[j], ret_decay_b[j], q_l, k_l, v_l, k_c, v_c, seq, ctx_len, "retention")
            mix_params = [pool_w[j].astype(BF16), pool_scale[j].reshape(1, pw), ret_gn_g[j].reshape(1, HEAD_V)]
            xs = _mixer_ffn("odd", xs, [pv, gate, o_l], mod_l, w_out, mix_params, norm2_g[i], *ffn_w, seq, tm_ffn_lat,
                            fc, 1.0, "mix_ffn_odd_lat")
    return xs.reshape(batch, seq, d)
```

```python
import functools
import math

import jax
import jax.numpy as jnp
from jax import lax
from jax.experimental import pallas as pl
from jax.experimental.pallas import tpu as pltpu

F32 = jnp.float32
BF16 = jnp.bfloat16

EPS = 1e-6
GRID_W = 64
ROPE_HEAD_DIM = 64
ROPE_FREQS = ROPE_HEAD_DIM // 4
ROPE_BASE = 10000.0
HEADS = 4
HEAD_V = 128
RET_CHUNK = 128
POOL_RADII = (1, 2, 4, 8)
LANES = 128
HALO = 16
VMEM_LIMIT = 56 * 1024 * 1024
LOG2E = math.log2(math.e)
BOUND_MARGIN = 1.02
MAX_SAFE_SCORE_BOUND = 40.0
ATTN_ROWS = 2048


def _params(sem, vmem=VMEM_LIMIT):
    return pltpu.CompilerParams(dimension_semantics=sem, vmem_limit_bytes=vmem)


def _sigmoid(x):
    return 1.0 / (1.0 + jnp.exp(-x))


def _dot(a, b):
    return jnp.dot(a, b, preferred_element_type=F32)


def _dot_nt(a, b):
    return lax.dot_general(a, b, (((1,), (1,)), ((), ())), preferred_element_type=F32)


def _dot_tn(a, b):
    return lax.dot_general(a, b, (((0,), (0,)), ((), ())), preferred_element_type=F32)


def _rms(x, width):
    return x * lax.rsqrt(jnp.sum(x * x, axis=-1, keepdims=True) * (1.0 / width) + EPS)


def _norm_mod(x, g, shift, scale):
    return (_rms(x, x.shape[-1]) * g) * (1.0 + scale) + shift


def _half_mask(shape):
    return lax.broadcasted_iota(jnp.int32, shape, len(shape) - 1) < (LANES // 2)


def _rope(t, cos, sin):
    lane = lax.broadcasted_iota(jnp.int32, t.shape, 1)
    first = lax.rem(lane, 2 * ROPE_FREQS) < ROPE_FREQS
    partner = jnp.where(first, pltpu.roll(t, LANES - ROPE_FREQS, axis=1), pltpu.roll(t, ROPE_FREQS, axis=1))
    return t * cos + partner * sin


def _norm64(t, gain):
    lo = _half_mask(t.shape)
    sq = t * t
    s_lo = jnp.sum(jnp.where(lo, sq, 0.0), axis=-1, keepdims=True)
    s_hi = jnp.sum(jnp.where(lo, 0.0, sq), axis=-1, keepdims=True)
    w = 1.0 / (LANES // 2)
    r = jnp.where(lo, lax.rsqrt(s_lo * w + EPS), lax.rsqrt(s_hi * w + EPS))
    return (t * r) * gain


def _mod_kernel(c_ref, w_ref, b_ref, o_ref):
    c = c_ref[...]
    a = c * _sigmoid(c)
    o_ref[...] = _dot(a.astype(BF16), w_ref[...].astype(BF16)) + b_ref[...]


def _modulation(c_all, mod_w, mod_b):
    depth, d, n6 = mod_w.shape
    rows = c_all.shape[0]
    tn = 1536
    return pl.pallas_call(
        _mod_kernel,
        grid=(depth, n6 // tn),
        in_specs=[
            pl.BlockSpec((rows, d), lambda l, j: (0, 0)),
            pl.BlockSpec((None, d, tn), lambda l, j: (l, 0, j)),
            pl.BlockSpec((None, 1, tn), lambda l, j: (l, 0, j)),
        ],
        out_specs=pl.BlockSpec((None, rows, tn), lambda l, j: (l, 0, j)),
        out_shape=jax.ShapeDtypeStruct((depth, rows, n6), F32),
        compiler_params=_params(("parallel", "parallel")),
        name="modulation",
    )(c_all, mod_w, mod_b.reshape(depth, 1, n6))


def _in_proj_kernel(*refs, segs, rotate, n_gain):
    x_ref, mod_ref, g_ref, w_ref = refs[:4]
    gain_refs = refs[4:4 + n_gain]
    pos = 4 + n_gain
    if rotate:
        cos_ref, sin_ref = refs[pos:pos + 2]
        pos += 2
    out_refs = refs[pos:]
    h = _norm_mod(x_ref[...], g_ref[...], mod_ref[0:1, :], mod_ref[1:2, :]).astype(BF16)
    for (c0, c1, mode, gain_idx, scale), o_ref in zip(segs, out_refs):
        y = _dot(h, w_ref[:, c0:c1])
        if mode == "copy":
            o_ref[...] = y.astype(o_ref.dtype)
            continue
        for s in range((c1 - c0) // LANES):
            t = y[:, s * LANES:(s + 1) * LANES]
            if mode == "qknorm":
                t = _norm64(t, gain_refs[gain_idx][...])
            if rotate:
                t = _rope(t, cos_ref[...], sin_ref[...])
            if scale != 1.0:
                t = t * scale
            o_ref[:, s * LANES:(s + 1) * LANES] = t.astype(o_ref.dtype)


def _in_proj(x, mod, g, w, segs, out_dtypes, gains, rope, seq_len, tm, name):
    t_rows, d = x.shape
    per_seq = mod.shape[0] > 1
    tiles_per_seq = max(seq_len // tm, 1)
    if per_seq:
        assert seq_len % tm == 0
        mod_map = lambda i: (i // tiles_per_seq, 0, 0)
    else:
        mod_map = lambda i: (0, 0, 0)
    in_specs = [
        pl.BlockSpec((tm, d), lambda i: (i, 0)),
        pl.BlockSpec((None, 6, d), mod_map),
        pl.BlockSpec((1, d), lambda i: (0, 0)),
        pl.BlockSpec(w.shape, lambda i: (0, 0)),
    ]
    args = [x, mod, g.reshape(1, d), w]
    for gn in gains:
        in_specs.append(pl.BlockSpec((1, LANES), lambda i: (0, 0)))
        args.append(jnp.tile(gn.reshape(1, -1), (1, LANES // gn.shape[-1])))
    if rope is not None:
        assert seq_len % tm == 0
        for tab in rope:
            in_specs.append(pl.BlockSpec((tm, LANES), lambda i: (i % tiles_per_seq, 0)))
            args.append(tab)
    out_specs = [pl.BlockSpec((tm, c1 - c0), lambda i: (i, 0)) for (c0, c1, _, _, _) in segs]
    out_shape = [jax.ShapeDtypeStruct((t_rows, c1 - c0), dt) for (c0, c1, _, _, _), dt in zip(segs, out_dtypes)]
    return pl.pallas_call(
        functools.partial(_in_proj_kernel, segs=tuple(segs), rotate=rope is not None, n_gain=len(gains)),
        grid=(t_rows // tm,),
        in_specs=in_specs,
        out_specs=out_specs,
        out_shape=out_shape,
        compiler_params=_params(("parallel",)),
        name=name,
    )(*args)


def _diff_attn_kernel(*refs, n_src, lam_init):
    bound_ref, lam_ref, q_ref = refs[:3]
    kv_refs = refs[3:3 + 2 * n_src]
    o_ref = refs[3 + 2 * n_src]
    lv = lam_ref[...]
    lam = (jnp.exp(jnp.sum(lv[0:1] * lv[1:2], axis=-1, keepdims=True))
           - jnp.exp(jnp.sum(lv[2:3] * lv[3:4], axis=-1, keepdims=True)) + lam_init)
    bound = bound_ref[0]

    def attend_head(cols, use_bound):
        q = q_ref[:, cols]
        ks = [kv_refs[2 * j][:, cols] for j in range(n_src)]
        vs = [kv_refs[2 * j + 1][:, cols] for j in range(n_src)]

        def softmax_av(first_map):
            ss = []
            for k in ks:
                lo = _half_mask(k.shape)
                zero = jnp.zeros_like(k)
                ss.append(_dot_nt(jnp.where(lo, k, zero) if first_map else jnp.where(lo, zero, k), q))
            if use_bound:
                m = bound
            else:
                m = functools.reduce(jnp.maximum, [jnp.max(s, axis=0, keepdims=True) for s in ss])
            ps = [jnp.exp2(s - m) for s in ss]
            l = functools.reduce(jnp.add, [jnp.sum(p, axis=0, keepdims=True) for p in ps])
            acc = functools.reduce(jnp.add, [_dot_tn(v, p.astype(BF16)) for v, p in zip(vs, ps)])
            return acc, l

        acc0, l0 = softmax_av(True)
        acc1, l1 = softmax_av(False)
        o_t = acc0 * (1.0 / l0) - acc1 * (lam / l1)
        o_ref[:, cols] = o_t.T.astype(o_ref.dtype)

    def attend(use_bound):
        for hd in range(q_ref.shape[1] // HEAD_V):
            attend_head(slice(hd * HEAD_V, (hd + 1) * HEAD_V), use_bound)

    safe = bound <= MAX_SAFE_SCORE_BOUND

    @pl.when(safe)
    def _():
        attend(True)

    @pl.when(jnp.logical_not(safe))
    def _():
        attend(False)


def _diff_attn(bound, lam_vecs, q, kvs, seq_q, tq, heads_per_step, lam_init, name):
    t_rows, width = q.shape
    batch = t_rows // seq_q
    nq = seq_q // tq
    bw = heads_per_step * HEAD_V
    in_specs = [
        pl.BlockSpec(memory_space=pltpu.SMEM),
        pl.BlockSpec(lam_vecs.shape, lambda b, h, i: (0, 0)),
        pl.BlockSpec((tq, bw), lambda b, h, i: (b * nq + i, h)),
    ]
    args = [bound, lam_vecs, q]
    for k, v, seq_k in kvs:
        in_specs.append(pl.BlockSpec((seq_k, bw), lambda b, h, i: (b, h)))
        in_specs.append(pl.BlockSpec((seq_k, bw), lambda b, h, i: (b, h)))
        args += [k, v]
    return pl.pallas_call(
        functools.partial(_diff_attn_kernel, n_src=len(kvs), lam_init=lam_init),
        grid=(batch, width // bw, nq),
        in_specs=in_specs,
        out_specs=pl.BlockSpec((tq, bw), lambda b, h, i: (b * nq + i, h)),
        out_shape=jax.ShapeDtypeStruct((t_rows, width), BF16),
        compiler_params=_params(("parallel", "parallel", "parallel")),
        name=name,
    )(*args)


def _three(arr, tm):
    t_rows, width = arr.shape
    per = tm // HALO
    last = t_rows // HALO - 1
    prev = pl.BlockSpec((HALO, width), lambda i: (jnp.maximum(i * per - 1, 0), 0))
    nxt = pl.BlockSpec((HALO, width), lambda i: (jnp.minimum((i + 1) * per, last), 0))
    return [(arr, prev), (arr, pl.BlockSpec((tm, width), lambda i: (i, 0))), (arr, nxt)]


def _cat_rows(refs):
    return jnp.concatenate([r[...] for r in refs], axis=0)


def _tile_rows(tm, seq_len):
    row = lax.broadcasted_iota(jnp.int32, (tm + 2 * HALO, 1), 0)
    if tm > seq_len:
        return lax.rem(pl.program_id(0) * tm + (seq_len - HALO) + row, seq_len), None
    pos = lax.rem(pl.program_id(0) * tm, seq_len) - HALO + row
    return pos, jnp.logical_and(pos >= 0, pos < seq_len).astype(F32)


def _shift_rows(u, pos, own, seq_len):
    rows = u.shape[0]
    dn = pltpu.roll(u, 1, axis=0)
    up = pltpu.roll(u, rows - 1, axis=0)
    if own is None:
        dn = jnp.where(pos == 0, 0.0, dn)
        up = jnp.where(pos == seq_len - 1, 0.0, up)
    return dn, up


def _head_rms(o, gain, post_scale=1.0):
    parts = []
    for h in range(o.shape[-1] // HEAD_V):
        t = _rms(o[:, h * HEAD_V:(h + 1) * HEAD_V].astype(F32), HEAD_V) * gain
        parts.append(t * post_scale if post_scale != 1.0 else t)
    return parts


def _even_mix(gate_refs, o_refs, cw_ref, sg_ref, pos, own, seq_len, post_scale):
    gates = _cat_rows(gate_refs).astype(F32)
    w = gates.shape[1] // 3
    u = gates[:, w:2 * w] * gates[:, 2 * w:3 * w]
    if own is not None:
        u = u * own
    dn, up = _shift_rows(u, pos, own, seq_len)
    y_conv = gates[:, 0:w] * (dn * cw_ref[0:1, :] + u * cw_ref[1:2, :] + up * cw_ref[2:3, :])
    heads = _head_rms(_cat_rows(o_refs), sg_ref[...], post_scale)
    return jnp.concatenate([y_conv.astype(BF16)] + [p.astype(BF16) for p in heads], axis=-1)


def _odd_mix(pv_refs, gate_refs, o_refs, pw_ref, ps_ref, gg_ref, pos, own, seq_len):
    assert own is not None and max(POOL_RADII) < HALO
    pv = _cat_rows(pv_refs).astype(F32) * own
    rows = pv.shape[0]
    parts = []
    for gi, r in enumerate(POOL_RADII):
        v = pv[:, gi * LANES:(gi + 1) * LANES]
        win = v
        span = 1
        while span < 2 * r:
            win = win + pltpu.roll(win, rows - span, axis=0)
            span *= 2
        total = pltpu.roll(win, r, axis=0) + pltpu.roll(v, rows - r, axis=0)
        cnt = jnp.maximum(jnp.minimum(pos + r + 1, seq_len) - jnp.maximum(pos - r, 0), 1).astype(F32)
        y = (total / cnt - v).astype(BF16)
        parts.append(_dot(y, pw_ref[gi]))
    y_p = jnp.concatenate(parts, axis=-1) * ps_ref[...]
    gate = _cat_rows(gate_refs).astype(F32)
    y_r = jnp.concatenate(_head_rms(_cat_rows(o_refs), gg_ref[...]), axis=-1) * (gate * _sigmoid(gate))
    return jnp.concatenate([y_p.astype(BF16), y_r.astype(BF16)], axis=-1)


def _mixer_ffn_kernel(*refs, kind, n_mix, seq_len, fc, post_scale):
    x_refs = refs[0:3]
    mix_refs = [refs[3 + 3 * k:6 + 3 * k] for k in range(n_mix)]
    p = 3 + 3 * n_mix
    mod_ref, n2_ref, wo_ref = refs[p:p + 3]
    n_par = 2 if kind == "even" else 3
    mix_par = refs[p + 3:p + 3 + n_par]
    wu_ref, cw_ref, cb_ref, wd_ref, y_ref, h_ref, act_ref = refs[p + 3 + n_par:]
    tm = y_ref.shape[0]
    rows = tm + 2 * HALO
    fdim = wd_ref.shape[0]
    pos, own = _tile_rows(tm, seq_len)

    if kind == "even":
        z = _even_mix(mix_refs[0], mix_refs[1], mix_par[0], mix_par[1], pos, own, seq_len, post_scale)
    else:
        z = _odd_mix(mix_refs[0], mix_refs[1], mix_refs[2], mix_par[0], mix_par[1], mix_par[2], pos, own, seq_len)
    x1 = _cat_rows(x_refs) + mod_ref[2:3, :] * _dot(z, wo_ref[...])
    y_ref[...] = x1[HALO:HALO + tm]
    h = _norm_mod(x1, n2_ref[...], mod_ref[3:4, :], mod_ref[4:5, :])
    if own is not None:
        h = h * own
    h_ref[...] = h.astype(BF16)

    for j in range(fdim // fc):
        cols = slice(j * fc, (j + 1) * fc)
        a_ext = _dot(h_ref[...], wu_ref[:, cols])
        dn = pltpu.roll(a_ext, 1, axis=0)[HALO:HALO + tm]
        up = pltpu.roll(a_ext, rows - 1, axis=0)[HALO:HALO + tm]
        if own is None:
            dn = jnp.where(pos[HALO:HALO + tm] == 0, 0.0, dn)
            up = jnp.where(pos[HALO:HALO + tm] == seq_len - 1, 0.0, up)
        a = (dn * cw_ref[0:1, cols] + a_ext[HALO:HALO + tm] * cw_ref[1:2, cols] + up * cw_ref[2:3, cols]
             + cb_ref[:, cols])
        b = _dot(h_ref[HALO:HALO + tm, :], wu_ref[:, fdim + j * fc:fdim + (j + 1) * fc])
        act_ref[:, cols] = (a * _sigmoid(a) * b).astype(BF16)
    y_ref[...] += mod_ref[5:6, :] * _dot(act_ref[...], wd_ref[...])


def _mixer_ffn(kind, x, mix_arrays, mod, w_out, mix_params, norm_g, layer, w_up, conv_w, conv_b, w_down, seq_len, tm,
               fc, post_scale, name):
    t_rows, d = x.shape
    fdim = w_down.shape[1]
    per_seq = mod.shape[0] > 1
    assert tm % HALO == 0 and (seq_len % tm == 0 or tm % seq_len == 0)
    assert seq_len % tm == 0 or not per_seq
    tiles_per_seq = max(seq_len // tm, 1)
    mod_map = (lambda i: (i // tiles_per_seq, 0, 0)) if per_seq else (lambda i: (0, 0, 0))

    def whole(arr, single=False):
        mode = dict(pipeline_mode=pl.Buffered(1)) if single else {}
        return arr, pl.BlockSpec(arr.shape, lambda i: (0,) * arr.ndim, **mode)

    def of_layer(arr, single=False):
        mode = dict(pipeline_mode=pl.Buffered(1)) if single else {}
        return arr, pl.BlockSpec((None,) + arr.shape[1:], lambda i: (layer,) + (0,) * (arr.ndim - 1), **mode)

    operands = _three(x, tm)
    for arr in mix_arrays:
        operands += _three(arr, tm)
    operands += [(mod, pl.BlockSpec((None, 6, d), mod_map)), whole(norm_g.reshape(1, d)), whole(w_out, True)]
    operands += [whole(a) for a in mix_params]
    operands += [of_layer(w_up, True), of_layer(conv_w), of_layer(conv_b.reshape(-1, 1, fdim)), of_layer(w_down, True)]
    return pl.pallas_call(
        functools.partial(_mixer_ffn_kernel, kind=kind, n_mix=len(mix_arrays), seq_len=seq_len, fc=fc,
                          post_scale=post_scale),
        grid=(t_rows // tm,),
        in_specs=[spec for _, spec in operands],
        out_specs=pl.BlockSpec((tm, d), lambda i: (i, 0)),
        out_shape=jax.ShapeDtypeStruct((t_rows, d), F32),
        scratch_shapes=[pltpu.VMEM((tm + 2 * HALO, d), BF16),
                        pltpu.VMEM((tm, fdim), BF16)],
        compiler_params=_params(("parallel",)),
        name=name,
    )(*[arr for arr, _ in operands])


def _retention_kernel(df_ref, db_ref, q_ref, k_ref, v_ref, kc_ref, vc_ref, o_ref, kv_ref, ss_ref):
    n = q_ref.shape[0]
    lc = kc_ref.shape[0]
    c = RET_CHUNK
    nc = n // c
    pos = lax.broadcasted_iota(jnp.int32, (c, 1), 0).astype(F32)
    ii = lax.broadcasted_iota(jnp.int32, (c, c), 0)
    jj = lax.broadcasted_iota(jnp.int32, (c, c), 1)
    dij = (ii - jj).astype(F32)
    jpos = lax.broadcasted_iota(jnp.int32, (lc, 1), 0).astype(F32)
    lo = _half_mask((1, LANES))
    row_lo = lax.broadcasted_iota(jnp.int32, (LANES, 1), 0) < (LANES // 2)
    block_diag = (lax.broadcasted_iota(jnp.int32, (LANES, 2 * HEAD_V), 0) // (LANES // 2)
                  == lax.broadcasted_iota(jnp.int32, (LANES, 2 * HEAD_V), 1) // HEAD_V)
    zeros_v = jnp.zeros((c, HEAD_V), BF16)

    def cat(a, b):
        return jnp.concatenate([a, b], axis=-1)

    for pair in range(HEADS // 2):
        slab = slice(pair * LANES, (pair + 1) * LANES)
        vcol = slice(pair * 2 * HEAD_V, (pair + 1) * 2 * HEAD_V)
        lg = []
        for d_ref in (df_ref, db_ref):
            lg.append([-jnp.exp(jnp.full((1, 1), d_ref[2 * pair + e], F32)) for e in range(2)])
        (lgf0, lgf1), (lgb0, lgb1) = lg
        lgf_lane = jnp.where(lo, lgf0, lgf1)
        lgb_lane = jnp.where(lo, lgb0, lgb1)
        lgf_row = jnp.where(row_lo, lgf0, lgf1)
        lgb_row = jnp.where(row_lo, lgb0, lgb1)
        qdec = cat(jnp.exp(lgf_lane * (pos + 1.0)), jnp.exp(lgb_lane * (c - pos)))
        kdec = cat(jnp.exp(lgf_lane * (c - 1.0 - pos)), jnp.exp(lgb_lane * pos))
        cdec_f = jnp.exp(lgf_row * float(c))
        cdec_b = jnp.exp(lgb_row * float(c))
        intra = [jnp.where(dij >= 0, jnp.exp(f * jnp.maximum(dij, 0.0)), 0.0)
                 + jnp.where(dij <= 0, jnp.exp(b * jnp.maximum(-dij, 0.0)), 0.0)
                 for f, b in ((lgf0, lgb0), (lgf1, lgb1))]

        for ci in range(nc):
            rows = slice(ci * c, (ci + 1) * c)
            k = k_ref[rows, slab].astype(F32)
            kv_ref[ci] = _dot_tn((cat(k, k) * kdec).astype(BF16), v_ref[rows, vcol].astype(BF16))
        kc = kc_ref[:, slab].astype(F32)
        cdec0 = cat(jnp.exp(lgf_lane * (lc - 1.0 - jpos)), jnp.exp(lgb_lane * jpos))
        s0 = _dot_tn((cat(kc, kc) * cdec0).astype(BF16), vc_ref[:, vcol].astype(BF16))

        s = s0[0:LANES]
        for ci in range(nc):
            ss_ref[ci, 0:LANES, :] = jnp.where(block_diag, s, 0.0).astype(BF16)
            s = s * cdec_f + kv_ref[ci, 0:LANES, :]
        s = s0[LANES:2 * LANES]
        for ci in reversed(range(nc)):
            ss_ref[ci, LANES:2 * LANES, :] = jnp.where(block_diag, s, 0.0).astype(BF16)
            s = s * cdec_b + kv_ref[ci, LANES:2 * LANES, :]

        for ci in range(nc):
            rows = slice(ci * c, (ci + 1) * c)
            q = q_ref[rows, slab].astype(F32)
            kb = k_ref[rows, slab].astype(BF16)
            v = v_ref[rows, vcol].astype(BF16)
            att0 = _dot_nt(jnp.where(lo, q, 0.0).astype(BF16), kb) * intra[0]
            att1 = _dot_nt(jnp.where(lo, 0.0, q).astype(BF16), kb) * intra[1]
            v_blocks = jnp.concatenate([cat(v[:, 0:HEAD_V], zeros_v), cat(zeros_v, v[:, HEAD_V:])], axis=0)
            o_ref[rows, vcol] = (_dot(cat(att0, att1).astype(BF16), v_blocks)
                                 + _dot((cat(q, q) * qdec).astype(BF16), ss_ref[ci])).astype(o_ref.dtype)


def _retention(dec_f, dec_b, q, k, v, kc, vc, seq_len, ctx_len, name):
    t_rows = q.shape[0]
    batch = t_rows // seq_len
    qw, vw = q.shape[1], v.shape[1]
    nc = seq_len // RET_CHUNK
    smem = pl.BlockSpec(memory_space=pltpu.SMEM)
    return pl.pallas_call(
        _retention_kernel,
        grid=(batch,),
        in_specs=[
            smem, smem,
            pl.BlockSpec((seq_len, qw), lambda b: (b, 0)),
            pl.BlockSpec((seq_len, qw), lambda b: (b, 0)),
            pl.BlockSpec((seq_len, vw), lambda b: (b, 0)),
            pl.BlockSpec((ctx_len, qw), lambda b: (b, 0)),
            pl.BlockSpec((ctx_len, vw), lambda b: (b, 0)),
        ],
        out_specs=pl.BlockSpec((seq_len, vw), lambda b: (b, 0)),
        out_shape=jax.ShapeDtypeStruct((t_rows, vw), BF16),
        scratch_shapes=[pltpu.VMEM((nc, 2 * LANES, 2 * HEAD_V), F32),
                        pltpu.VMEM((nc, 2 * LANES, 2 * HEAD_V), BF16)],
        compiler_params=_params(("parallel",)),
        name=name,
    )(dec_f, dec_b, q, k, v, kc, vc)


def _rope_tables(seq_len):
    rows = seq_len // GRID_W
    row = jnp.repeat(jnp.arange(rows, dtype=F32), GRID_W)
    col = jnp.tile(jnp.arange(GRID_W, dtype=F32), rows)
    inv = ROPE_BASE ** (-jnp.arange(ROPE_FREQS, dtype=F32) / ROPE_FREQS)
    ar, ac = row[:, None] * inv, col[:, None] * inv
    cos = jnp.concatenate([jnp.cos(ar), jnp.cos(ar), jnp.cos(ac), jnp.cos(ac)], axis=-1)
    sin = jnp.concatenate([-jnp.sin(ar), jnp.sin(ar), -jnp.sin(ac), jnp.sin(ac)], axis=-1)
    reps = LANES // ROPE_HEAD_DIM
    return jnp.tile(cos, (1, reps)), jnp.tile(sin, (1, reps))


def _tile(total, want):
    t = min(want, total)
    assert total % t == 0
    return t


def kernel(x, c, ctx, c_ctx, mod_w, mod_b, norm1_g, norm2_g, ev_w_in, ev_w_out, sc_conv_w, da_q_norm, da_k_norm,
           da_lq1, da_lk1, da_lq2, da_lk2, da_subln_g, od_w_in, od_w_out, pool_w, pool_scale, ret_decay_f,
           ret_decay_b, ret_gn_g, ffn_w_up, ffn_conv_w, ffn_conv_b, ffn_w_down):
    batch, seq, d = x.shape
    ctx_len = ctx.shape[1]
    depth = mod_w.shape[0]
    t_lat, t_ctx = batch * seq, batch * ctx_len
    xs = x.reshape(t_lat, d)
    cs = ctx.reshape(t_ctx, d)

    pad = (-(batch + 1)) % 8
    c_all = jnp.concatenate([c, c_ctx[None, :], jnp.zeros((pad, d), F32)], axis=0)
    mods = _modulation(c_all, mod_w, mod_b)
    rope = _rope_tables(seq)

    tm_lat = _tile(seq, 1024)
    tm_ctx = _tile(t_ctx, 512)
    tm_ffn_lat = _tile(seq, 1024)
    tm_ffn_ctx = _tile(t_ctx, 512)
    tq = _tile(seq, ATTN_ROWS)
    fdim = ffn_w_down.shape[1]
    fc = 256 if fdim % 256 == 0 else LANES
    ffn_up, ffn_down = ffn_w_up.astype(BF16), ffn_w_down.astype(BF16)

    for i in range(depth):
        last = i == depth - 1
        j = i // 2
        mod_l = mods[i, :batch].reshape(batch, 6, d)
        mod_c = mods[i, batch:batch + 1].reshape(1, 6, d)
        ffn_w = (i, ffn_up, ffn_conv_w, ffn_conv_b, ffn_down)
        if i % 2 == 0:
            w_in = ev_w_in[j].astype(BF16)
            w_out = ev_w_out[j].astype(BF16)
            scw = ev_w_out.shape[1] // 2
            qkw = (w_in.shape[1] - 4 * scw) // 2
            q_scale = ROPE_HEAD_DIM ** -0.5 * LOG2E
            segs = [(3 * scw, 3 * scw + qkw, "qknorm", 0, q_scale),
                    (3 * scw + qkw, 3 * scw + 2 * qkw, "qknorm", 1, 1.0),
                    (0, 3 * scw, "copy", 0, 1.0),
                    (3 * scw + 2 * qkw, w_in.shape[1], "copy", 0, 1.0)]
            dts = [BF16, BF16, BF16, BF16]
            gains = [da_q_norm[j], da_k_norm[j]]
            lam_init = 0.8 - 0.6 * math.exp(-0.3 * i)
            lam_vecs = jnp.stack([da_lq1[j], da_lk1[j], da_lq2[j], da_lk2[j]], axis=0)
            bound = (ROPE_HEAD_DIM * q_scale * BOUND_MARGIN * jnp.max(jnp.abs(da_q_norm[j]))
                     * jnp.max(jnp.abs(da_k_norm[j]))).reshape(1).astype(F32)
            q_l, k_l, g_l, v_l = _in_proj(xs, mod_l, norm1_g[i], w_in, segs, dts, gains, rope, seq, tm_lat,
                                          "in_even_lat")
            q_c, k_c, g_c, v_c = _in_proj(cs, mod_c, norm1_g[i], w_in, segs, dts, gains, None, ctx_len, tm_ctx,
                                          "in_even_ctx")
            o_l = _diff_attn(bound, lam_vecs, q_l, [(k_l, v_l, seq), (k_c, v_c, ctx_len)], seq, tq, 1, lam_init,
                             "attn_lat")
            mix_params = [sc_conv_w[j], da_subln_g[j].reshape(1, HEAD_V)]
            xs = _mixer_ffn("even", xs, [g_l, o_l], mod_l, w_out, mix_params, norm2_g[i], *ffn_w, seq, tm_ffn_lat, fc,
                            1.0 - lam_init, "mix_ffn_even_lat")
            if not last:
                o_c = _diff_attn(bound, lam_vecs, q_c, [(k_c, v_c, ctx_len)], ctx_len, ctx_len, HEADS, lam_init,
                                 "attn_ctx")
                cs = _mixer_ffn("even", cs, [g_c, o_c], mod_c, w_out, mix_params, norm2_g[i], *ffn_w, ctx_len,
                                tm_ffn_ctx, fc, 1.0 - lam_init, "mix_ffn_even_ctx")
        else:
            assert last, "an odd layer that still updates the context stream is not implemented"
            w_in = od_w_in[j].astype(BF16)
            w_out = od_w_out[j].astype(BF16)
            pw = pool_scale.shape[1]
            vw = HEADS * HEAD_V
            qw = (w_in.shape[1] - pw - 2 * vw) // 2
            segs = [(pw, pw + qw, "rope", 0, ROPE_HEAD_DIM ** -0.5),
                    (pw + qw, pw + 2 * qw, "rope", 0, 1.0),
                    (0, pw, "copy", 0, 1.0),
                    (pw + 2 * qw, pw + 2 * qw + vw, "copy", 0, 1.0),
                    (pw + 2 * qw + vw, w_in.shape[1], "copy", 0, 1.0)]
            q_l, k_l, pv, v_l, gate = _in_proj(xs, mod_l, norm1_g[i], w_in, segs, [BF16] * 5, [], rope, seq, tm_lat,
                                               "in_odd_lat")
            w_kv = w_in[:, pw + qw:pw + 2 * qw + vw]
            k_c, v_c = _in_proj(cs, mod_c, norm1_g[i], w_kv, [(0, qw, "copy", 0, 1.0), (qw, qw + vw, "copy", 0, 1.0)],
                                [BF16, BF16], [], None, ctx_len, tm_ctx, "in_odd_ctx")
            o_l = _retention(ret_decay_f[j], ret_decay_b[j], q_l, k_l, v_l, k_c, v_c, seq, ctx_len, "retention")
            mix_params = [pool_w[j].astype(BF16), pool_scale[j].reshape(1, pw), ret_gn_g[j].reshape(1, HEAD_V)]
            xs = _mixer_ffn("odd", xs, [pv, gate, o_l], mod_l, w_out, mix_params, norm2_g[i], *ffn_w, seq, tm_ffn_lat,
                            fc, 1.0, "mix_ffn_odd_lat")
    return xs.reshape(batch, seq, d)
```

```python
import functools
import math

import jax
import jax.numpy as jnp
from jax import lax
from jax.experimental import pallas as pl
from jax.experimental.pallas import tpu as pltpu

F32 = jnp.float32
BF16 = jnp.bfloat16

EPS = 1e-6
GRID_W = 64
ROPE_HEAD_DIM = 64
ROPE_FREQS = ROPE_HEAD_DIM // 4
ROPE_BASE = 10000.0
HEADS = 4
HEAD_V = 128
RET_CHUNK = 128
POOL_RADII = (1, 2, 4, 8)
LANES = 128
HALO = 16
VMEM_LIMIT = 56 * 1024 * 1024
LOG2E = math.log2(math.e)
BOUND_MARGIN = 1.02
MAX_SAFE_SCORE_BOUND = 40.0
ATTN_ROWS = 2048


def _params(sem, vmem=VMEM_LIMIT):
    return pltpu.CompilerParams(dimension_semantics=sem, vmem_limit_bytes=vmem)


def _sigmoid(x):
    return 1.0 / (1.0 + jnp.exp(-x))


def _dot(a, b):
    return jnp.dot(a, b, preferred_element_type=F32)


def _dot_nt(a, b):
    return lax.dot_general(a, b, (((1,), (1,)), ((), ())), preferred_element_type=F32)


def _dot_tn(a, b):
    return lax.dot_general(a, b, (((0,), (0,)), ((), ())), preferred_element_type=F32)


def _rms(x, width):
    return x * lax.rsqrt(jnp.sum(x * x, axis=-1, keepdims=True) * (1.0 / width) + EPS)


def _norm_mod(x, g, shift, scale):
    return (_rms(x, x.shape[-1]) * g) * (1.0 + scale) + shift


def _half_mask(shape):
    return lax.broadcasted_iota(jnp.int32, shape, len(shape) - 1) < (LANES // 2)


def _rope(t, cos, sin):
    lane = lax.broadcasted_iota(jnp.int32, t.shape, 1)
    first = lax.rem(lane, 2 * ROPE_FREQS) < ROPE_FREQS
    partner = jnp.where(first, pltpu.roll(t, LANES - ROPE_FREQS, axis=1), pltpu.roll(t, ROPE_FREQS, axis=1))
    return t * cos + partner * sin


def _norm64(t, gain):
    lo = _half_mask(t.shape)
    sq = t * t
    s_lo = jnp.sum(jnp.where(lo, sq, 0.0), axis=-1, keepdims=True)
    s_hi = jnp.sum(jnp.where(lo, 0.0, sq), axis=-1, keepdims=True)
    w = 1.0 / (LANES // 2)
    r = jnp.where(lo, lax.rsqrt(s_lo * w + EPS), lax.rsqrt(s_hi * w + EPS))
    return (t * r) * gain


def _mod_kernel(c_ref, w_ref, b_ref, o_ref):
    c = c_ref[...]
    a = c * _sigmoid(c)
    o_ref[...] = _dot(a.astype(BF16), w_ref[...].astype(BF16)) + b_ref[...]


def _modulation(c_all, mod_w, mod_b):
    depth, d, n6 = mod_w.shape
    rows = c_all.shape[0]
    tn = 1536
    return pl.pallas_call(
        _mod_kernel,
        grid=(depth, n6 // tn),
        in_specs=[
            pl.BlockSpec((rows, d), lambda l, j: (0, 0)),
            pl.BlockSpec((None, d, tn), lambda l, j: (l, 0, j)),
            pl.BlockSpec((None, 1, tn), lambda l, j: (l, 0, j)),
        ],
        out_specs=pl.BlockSpec((None, rows, tn), lambda l, j: (l, 0, j)),
        out_shape=jax.ShapeDtypeStruct((depth, rows, n6), F32),
        compiler_params=_params(("parallel", "parallel")),
        name="modulation",
    )(c_all, mod_w, mod_b.reshape(depth, 1, n6))


def _in_proj_kernel(*refs, segs, rotate, n_gain):
    x_ref, mod_ref, g_ref, w_ref = refs[:4]
    gain_refs = refs[4:4 + n_gain]
    pos = 4 + n_gain
    if rotate:
        cos_ref, sin_ref = refs[pos:pos + 2]
        pos += 2
    out_refs = refs[pos:]
    h = _norm_mod(x_ref[...], g_ref[...], mod_ref[0:1, :], mod_ref[1:2, :]).astype(BF16)
    for (c0, c1, mode, gain_idx, scale), o_ref in zip(segs, out_refs):
        y = _dot(h, w_ref[:, c0:c1])
        if mode == "copy":
            o_ref[...] = y.astype(o_ref.dtype)
            continue
        for s in range((c1 - c0) // LANES):
            t = y[:, s * LANES:(s + 1) * LANES]
            if mode == "qknorm":
                t = _norm64(t, gain_refs[gain_idx][...])
            if rotate:
                t = _rope(t, cos_ref[...], sin_ref[...])
            if scale != 1.0:
                t = t * scale
            o_ref[:, s * LANES:(s + 1) * LANES] = t.astype(o_ref.dtype)


def _in_proj(x, mod, g, w, segs, out_dtypes, gains, rope, seq_len, tm, name):
    t_rows, d = x.shape
    per_seq = mod.shape[0] > 1
    tiles_per_seq = max(seq_len // tm, 1)
    if per_seq:
        assert seq_len % tm == 0
        mod_map = lambda i: (i // tiles_per_seq, 0, 0)
    else:
        mod_map = lambda i: (0, 0, 0)
    in_specs = [
        pl.BlockSpec((tm, d), lambda i: (i, 0)),
        pl.BlockSpec((None, 6, d), mod_map),
        pl.BlockSpec((1, d), lambda i: (0, 0)),
        pl.BlockSpec(w.shape, lambda i: (0, 0)),
    ]
    args = [x, mod, g.reshape(1, d), w]
    for gn in gains:
        in_specs.append(pl.BlockSpec((1, LANES), lambda i: (0, 0)))
        args.append(jnp.tile(gn.reshape(1, -1), (1, LANES // gn.shape[-1])))
    if rope is not None:
        assert seq_len % tm == 0
        for tab in rope:
            in_specs.append(pl.BlockSpec((tm, LANES), lambda i: (i % tiles_per_seq, 0)))
            args.append(tab)
    out_specs = [pl.BlockSpec((tm, c1 - c0), lambda i: (i, 0)) for (c0, c1, _, _, _) in segs]
    out_shape = [jax.ShapeDtypeStruct((t_rows, c1 - c0), dt) for (c0, c1, _, _, _), dt in zip(segs, out_dtypes)]
    return pl.pallas_call(
        functools.partial(_in_proj_kernel, segs=tuple(segs), rotate=rope is not None, n_gain=len(gains)),
        grid=(t_rows // tm,),
        in_specs=in_specs,
        out_specs=out_specs,
        out_shape=out_shape,
        compiler_params=_params(("parallel",)),
        name=name,
    )(*args)


def _diff_attn_kernel(*refs, n_src, lam_init):
    bound_ref, lam_ref, q_ref = refs[:3]
    kv_refs = refs[3:3 + 2 * n_src]
    o_ref = refs[3 + 2 * n_src]
    lv = lam_ref[...]
    lam = (jnp.exp(jnp.sum(lv[0:1] * lv[1:2], axis=-1, keepdims=True))
           - jnp.exp(jnp.sum(lv[2:3] * lv[3:4], axis=-1, keepdims=True)) + lam_init)
    bound = bound_ref[0]

    def attend_head(cols, use_bound):
        q = q_ref[:, cols]
        ks = [kv_refs[2 * j][:, cols] for j in range(n_src)]
        vs = [kv_refs[2 * j + 1][:, cols] for j in range(n_src)]

        def softmax_av(first_map):
            ss = []
            for k in ks:
                lo = _half_mask(k.shape)
                zero = jnp.zeros_like(k)
                ss.append(_dot_nt(jnp.where(lo, k, zero) if first_map else jnp.where(lo, zero, k), q))
            if use_bound:
                m = bound
            else:
                m = functools.reduce(jnp.maximum, [jnp.max(s, axis=0, keepdims=True) for s in ss])
            ps = [jnp.exp2(s - m) for s in ss]
            l = functools.reduce(jnp.add, [jnp.sum(p, axis=0, keepdims=True) for p in ps])
            acc = functools.reduce(jnp.add, [_dot_tn(v, p.astype(BF16)) for v, p in zip(vs, ps)])
            return acc, l

        acc0, l0 = softmax_av(True)
        acc1, l1 = softmax_av(False)
        o_t = acc0 * (1.0 / l0) - acc1 * (lam / l1)
        o_ref[:, cols] = o_t.T.astype(o_ref.dtype)

    def attend(use_bound):
        for hd in range(q_ref.shape[1] // HEAD_V):
            attend_head(slice(hd * HEAD_V, (hd + 1) * HEAD_V), use_bound)

    safe = bound <= MAX_SAFE_SCORE_BOUND

    @pl.when(safe)
    def _():
        attend(True)

    @pl.when(jnp.logical_not(safe))
    def _():
        attend(False)


def _diff_attn(bound, lam_vecs, q, kvs, seq_q, tq, heads_per_step, lam_init, name):
    t_rows, width = q.shape
    batch = t_rows // seq_q
    nq = seq_q // tq
    bw = heads_per_step * HEAD_V
    in_specs = [
        pl.BlockSpec(memory_space=pltpu.SMEM),
        pl.BlockSpec(lam_vecs.shape, lambda b, h, i: (0, 0)),
        pl.BlockSpec((tq, bw), lambda b, h, i: (b * nq + i, h)),
    ]
    args = [bound, lam_vecs, q]
    for k, v, seq_k in kvs:
        in_specs.append(pl.BlockSpec((seq_k, bw), lambda b, h, i: (b, h)))
        in_specs.append(pl.BlockSpec((seq_k, bw), lambda b, h, i: (b, h)))
        args += [k, v]
    return pl.pallas_call(
        functools.partial(_diff_attn_kernel, n_src=len(kvs), lam_init=lam_init),
        grid=(batch, width // bw, nq),
        in_specs=in_specs,
        out_specs=pl.BlockSpec((tq, bw), lambda b, h, i: (b * nq + i, h)),
        out_shape=jax.ShapeDtypeStruct((t_rows, width), BF16),
        compiler_params=_params(("parallel", "parallel", "parallel")),
        name=name,
    )(*args)


def _three(arr, tm):
    t_rows, width = arr.shape
    per = tm // HALO
    last = t_rows // HALO - 1
    prev = pl.BlockSpec((HALO, width), lambda i: (jnp.maximum(i * per - 1, 0), 0))
    nxt = pl.BlockSpec((HALO, width), lambda i: (jnp.minimum((i + 1) * per, last), 0))
    return [(arr, prev), (arr, pl.BlockSpec((tm, width), lambda i: (i, 0))), (arr, nxt)]


def _cat_rows(refs):
    return jnp.concatenate([r[...] for r in refs], axis=0)


def _tile_rows(tm, seq_len):
    row = lax.broadcasted_iota(jnp.int32, (tm + 2 * HALO, 1), 0)
    if tm > seq_len:
        return lax.rem(pl.program_id(0) * tm + (seq_len - HALO) + row, seq_len), None
    pos = lax.rem(pl.program_id(0) * tm, seq_len) - HALO + row
    return pos, jnp.logical_and(pos >= 0, pos < seq_len).astype(F32)


def _shift_rows(u, pos, own, seq_len):
    rows = u.shape[0]
    dn = pltpu.roll(u, 1, axis=0)
    up = pltpu.roll(u, rows - 1, axis=0)
    if own is None:
        dn = jnp.where(pos == 0, 0.0, dn)
        up = jnp.where(pos == seq_len - 1, 0.0, up)
    return dn, up


def _head_rms(o, gain, post_scale=1.0):
    parts = []
    for h in range(o.shape[-1] // HEAD_V):
        t = _rms(o[:, h * HEAD_V:(h + 1) * HEAD_V].astype(F32), HEAD_V) * gain
        parts.append(t * post_scale if post_scale != 1.0 else t)
    return parts


def _even_mix(gate_refs, o_refs, cw_ref, sg_ref, pos, own, seq_len, post_scale):
    gates = _cat_rows(gate_refs).astype(F32)
    w = gates.shape[1] // 3
    u = gates[:, w:2 * w] * gates[:, 2 * w:3 * w]
    if own is not None:
        u = u * own
    dn, up = _shift_rows(u, pos, own, seq_len)
    y_conv = gates[:, 0:w] * (dn * cw_ref[0:1, :] + u * cw_ref[1:2, :] + up * cw_ref[2:3, :])
    heads = _head_rms(_cat_rows(o_refs), sg_ref[...], post_scale)
    return jnp.concatenate([y_conv.astype(BF16)] + [p.astype(BF16) for p in heads], axis=-1)


def _odd_mix(pv_refs, gate_refs, o_refs, pw_ref, ps_ref, gg_ref, pos, own, seq_len):
    assert own is not None and max(POOL_RADII) < HALO
    pv = _cat_rows(pv_refs).astype(F32) * own
    pv_b = pv.astype(BF16)
    rows = pv.shape[0]
    assert (rows - 2 * HALO) % LANES == 0

    def band(m, n, off, r):
        i = lax.broadcasted_iota(jnp.int32, (m, n), 0)
        j = lax.broadcasted_iota(jnp.int32, (m, n), 1)
        return jnp.where(jnp.abs(i + off - j) <= r, 1.0, 0.0).astype(BF16)

    parts = []
    for gi, r in enumerate(POOL_RADII):
        v = pv[:, gi * LANES:(gi + 1) * LANES]
        vb = pv_b[:, gi * LANES:(gi + 1) * LANES]
        mid = band(LANES, LANES + 2 * HALO, HALO, r)
        sums = [_dot(band(HALO, 2 * HALO, 0, r), vb[0:2 * HALO])]
        for k in range((rows - 2 * HALO) // LANES):
            sums.append(_dot(mid, vb[k * LANES:(k + 1) * LANES + 2 * HALO]))
        sums.append(_dot(band(HALO, 2 * HALO, HALO, r), vb[rows - 2 * HALO:rows]))
        total = jnp.concatenate(sums, axis=0)
        cnt = jnp.maximum(jnp.minimum(pos + r + 1, seq_len) - jnp.maximum(pos - r, 0), 1).astype(F32)
        y = (total / cnt - v).astype(BF16)
        parts.append(_dot(y, pw_ref[gi]))
    y_p = jnp.concatenate(parts, axis=-1) * ps_ref[...]
    gate = _cat_rows(gate_refs).astype(F32)
    y_r = jnp.concatenate(_head_rms(_cat_rows(o_refs), gg_ref[...]), axis=-1) * (gate * _sigmoid(gate))
    return jnp.concatenate([y_p.astype(BF16), y_r.astype(BF16)], axis=-1)


def _mixer_ffn_kernel(*refs, kind, n_mix, seq_len, fc, post_scale):
    x_refs = refs[0:3]
    mix_refs = [refs[3 + 3 * k:6 + 3 * k] for k in range(n_mix)]
    p = 3 + 3 * n_mix
    mod_ref, n2_ref, wo_ref = refs[p:p + 3]
    n_par = 2 if kind == "even" else 3
    mix_par = refs[p + 3:p + 3 + n_par]
    wu_ref, cw_ref, cb_ref, wd_ref, y_ref, h_ref, act_ref = refs[p + 3 + n_par:]
    tm = y_ref.shape[0]
    rows = tm + 2 * HALO
    fdim = wd_ref.shape[0]
    pos, own = _tile_rows(tm, seq_len)

    if kind == "even":
        z = _even_mix(mix_refs[0], mix_refs[1], mix_par[0], mix_par[1], pos, own, seq_len, post_scale)
    else:
        z = _odd_mix(mix_refs[0], mix_refs[1], mix_refs[2], mix_par[0], mix_par[1], mix_par[2], pos, own, seq_len)
    x1 = _cat_rows(x_refs) + mod_ref[2:3, :] * _dot(z, wo_ref[...])
    y_ref[...] = x1[HALO:HALO + tm]
    h = _norm_mod(x1, n2_ref[...], mod_ref[3:4, :], mod_ref[4:5, :])
    if own is not None:
        h = h * own
    h_ref[...] = h.astype(BF16)

    for j in range(fdim // fc):
        cols = slice(j * fc, (j + 1) * fc)
        a_ext = _dot(h_ref[...], wu_ref[:, cols])
        dn = pltpu.roll(a_ext, 1, axis=0)[HALO:HALO + tm]
        up = pltpu.roll(a_ext, rows - 1, axis=0)[HALO:HALO + tm]
        if own is None:
            dn = jnp.where(pos[HALO:HALO + tm] == 0, 0.0, dn)
            up = jnp.where(pos[HALO:HALO + tm] == seq_len - 1, 0.0, up)
        a = (dn * cw_ref[0:1, cols] + a_ext[HALO:HALO + tm] * cw_ref[1:2, cols] + up * cw_ref[2:3, cols]
             + cb_ref[:, cols])
        b = _dot(h_ref[HALO:HALO + tm, :], wu_ref[:, fdim + j * fc:fdim + (j + 1) * fc])
        act_ref[:, cols] = (a * _sigmoid(a) * b).astype(BF16)
    y_ref[...] += mod_ref[5:6, :] * _dot(act_ref[...], wd_ref[...])


def _mixer_ffn(kind, x, mix_arrays, mod, w_out, mix_params, norm_g, layer, w_up, conv_w, conv_b, w_down, seq_len, tm,
               fc, post_scale, name):
    t_rows, d = x.shape
    fdim = w_down.shape[1]
    per_seq = mod.shape[0] > 1
    assert tm % HALO == 0 and (seq_len % tm == 0 or tm % seq_len == 0)
    assert seq_len % tm == 0 or not per_seq
    tiles_per_seq = max(seq_len // tm, 1)
    mod_map = (lambda i: (i // tiles_per_seq, 0, 0)) if per_seq else (lambda i: (0, 0, 0))

    def whole(arr, single=False):
        mode = dict(pipeline_mode=pl.Buffered(1)) if single else {}
        return arr, pl.BlockSpec(arr.shape, lambda i: (0,) * arr.ndim, **mode)

    def of_layer(arr, single=False):
        mode = dict(pipeline_mode=pl.Buffered(1)) if single else {}
        return arr, pl.BlockSpec((None,) + arr.shape[1:], lambda i: (layer,) + (0,) * (arr.ndim - 1), **mode)

    operands = _three(x, tm)
    for arr in mix_arrays:
        operands += _three(arr, tm)
    operands += [(mod, pl.BlockSpec((None, 6, d), mod_map)), whole(norm_g.reshape(1, d)), whole(w_out, True)]
    operands += [whole(a) for a in mix_params]
    operands += [of_layer(w_up, True), of_layer(conv_w), of_layer(conv_b.reshape(-1, 1, fdim)), of_layer(w_down, True)]
    return pl.pallas_call(
        functools.partial(_mixer_ffn_kernel, kind=kind, n_mix=len(mix_arrays), seq_len=seq_len, fc=fc,
                          post_scale=post_scale),
        grid=(t_rows // tm,),
        in_specs=[spec for _, spec in operands],
        out_specs=pl.BlockSpec((tm, d), lambda i: (i, 0)),
        out_shape=jax.ShapeDtypeStruct((t_rows, d), F32),
        scratch_shapes=[pltpu.VMEM((tm + 2 * HALO, d), BF16),
                        pltpu.VMEM((tm, fdim), BF16)],
        compiler_params=_params(("parallel",)),
        name=name,
    )(*[arr for arr, _ in operands])


def _retention_kernel(df_ref, db_ref, q_ref, k_ref, v_ref, kc_ref, vc_ref, o_ref, kv_ref, ss_ref):
    n = q_ref.shape[0]
    lc = kc_ref.shape[0]
    c = RET_CHUNK
    nc = n // c
    pos = lax.broadcasted_iota(jnp.int32, (c, 1), 0).astype(F32)
    ii = lax.broadcasted_iota(jnp.int32, (c, c), 0)
    jj = lax.broadcasted_iota(jnp.int32, (c, c), 1)
    dij = (ii - jj).astype(F32)
    jpos = lax.broadcasted_iota(jnp.int32, (lc, 1), 0).astype(F32)
    lo = _half_mask((1, LANES))
    row_lo = lax.broadcasted_iota(jnp.int32, (LANES, 1), 0) < (LANES // 2)
    block_diag = (lax.broadcasted_iota(jnp.int32, (LANES, 2 * HEAD_V), 0) // (LANES // 2)
                  == lax.broadcasted_iota(jnp.int32, (LANES, 2 * HEAD_V), 1) // HEAD_V)
    zeros_v = jnp.zeros((c, HEAD_V), BF16)

    def cat(a, b):
        return jnp.concatenate([a, b], axis=-1)

    for pair in range(HEADS // 2):
        slab = slice(pair * LANES, (pair + 1) * LANES)
        vcol = slice(pair * 2 * HEAD_V, (pair + 1) * 2 * HEAD_V)
        lg = []
        for d_ref in (df_ref, db_ref):
            lg.append([-jnp.exp(jnp.full((1, 1), d_ref[2 * pair + e], F32)) for e in range(2)])
        (lgf0, lgf1), (lgb0, lgb1) = lg
        lgf_lane = jnp.where(lo, lgf0, lgf1)
        lgb_lane = jnp.where(lo, lgb0, lgb1)
        lgf_row = jnp.where(row_lo, lgf0, lgf1)
        lgb_row = jnp.where(row_lo, lgb0, lgb1)
        qdec = cat(jnp.exp(lgf_lane * (pos + 1.0)), jnp.exp(lgb_lane * (c - pos)))
        kdec = cat(jnp.exp(lgf_lane * (c - 1.0 - pos)), jnp.exp(lgb_lane * pos))
        cdec_f = jnp.exp(lgf_row * float(c))
        cdec_b = jnp.exp(lgb_row * float(c))
        intra = [jnp.where(dij >= 0, jnp.exp(f * jnp.maximum(dij, 0.0)), 0.0)
                 + jnp.where(dij <= 0, jnp.exp(b * jnp.maximum(-dij, 0.0)), 0.0)
                 for f, b in ((lgf0, lgb0), (lgf1, lgb1))]

        for ci in range(nc):
            rows = slice(ci * c, (ci + 1) * c)
            k = k_ref[rows, slab].astype(F32)
            kv_ref[ci] = _dot_tn((cat(k, k) * kdec).astype(BF16), v_ref[rows, vcol].astype(BF16))
        kc = kc_ref[:, slab].astype(F32)
        cdec0 = cat(jnp.exp(lgf_lane * (lc - 1.0 - jpos)), jnp.exp(lgb_lane * jpos))
        s0 = _dot_tn((cat(kc, kc) * cdec0).astype(BF16), vc_ref[:, vcol].astype(BF16))

        s = s0[0:LANES]
        for ci in range(nc):
            ss_ref[ci, 0:LANES, :] = jnp.where(block_diag, s, 0.0).astype(BF16)
            s = s * cdec_f + kv_ref[ci, 0:LANES, :]
        s = s0[LANES:2 * LANES]
        for ci in reversed(range(nc)):
            ss_ref[ci, LANES:2 * LANES, :] = jnp.where(block_diag, s, 0.0).astype(BF16)
            s = s * cdec_b + kv_ref[ci, LANES:2 * LANES, :]

        for ci in range(nc):
            rows = slice(ci * c, (ci + 1) * c)
            q = q_ref[rows, slab].astype(F32)
            kb = k_ref[rows, slab].astype(BF16)
            v = v_ref[rows, vcol].astype(BF16)
            att0 = _dot_nt(jnp.where(lo, q, 0.0).astype(BF16), kb) * intra[0]
            att1 = _dot_nt(jnp.where(lo, 0.0, q).astype(BF16), kb) * intra[1]
            v_blocks = jnp.concatenate([cat(v[:, 0:HEAD_V], zeros_v), cat(zeros_v, v[:, HEAD_V:])], axis=0)
            o_ref[rows, vcol] = (_dot(cat(att0, att1).astype(BF16), v_blocks)
                                 + _dot((cat(q, q) * qdec).astype(BF16), ss_ref[ci])).astype(o_ref.dtype)


def _retention(dec_f, dec_b, q, k, v, kc, vc, seq_len, ctx_len, name):
    t_rows = q.shape[0]
    batch = t_rows // seq_len
    qw, vw = q.shape[1], v.shape[1]
    nc = seq_len // RET_CHUNK
    smem = pl.BlockSpec(memory_space=pltpu.SMEM)
    return pl.pallas_call(
        _retention_kernel,
        grid=(batch,),
        in_specs=[
            smem, smem,
            pl.BlockSpec((seq_len, qw), lambda b: (b, 0)),
            pl.BlockSpec((seq_len, qw), lambda b: (b, 0)),
            pl.BlockSpec((seq_len, vw), lambda b: (b, 0)),
            pl.BlockSpec((ctx_len, qw), lambda b: (b, 0)),
            pl.BlockSpec((ctx_len, vw), lambda b: (b, 0)),
        ],
        out_specs=pl.BlockSpec((seq_len, vw), lambda b: (b, 0)),
        out_shape=jax.ShapeDtypeStruct((t_rows, vw), BF16),
        scratch_shapes=[pltpu.VMEM((nc, 2 * LANES, 2 * HEAD_V), F32),
                        pltpu.VMEM((nc, 2 * LANES, 2 * HEAD_V), BF16)],
        compiler_params=_params(("parallel",)),
        name=name,
    )(dec_f, dec_b, q, k, v, kc, vc)


def _rope_tables(seq_len):
    rows = seq_len // GRID_W
    row = jnp.repeat(jnp.arange(rows, dtype=F32), GRID_W)
    col = jnp.tile(jnp.arange(GRID_W, dtype=F32), rows)
    inv = ROPE_BASE ** (-jnp.arange(ROPE_FREQS, dtype=F32) / ROPE_FREQS)
    ar, ac = row[:, None] * inv, col[:, None] * inv
    cos = jnp.concatenate([jnp.cos(ar), jnp.cos(ar), jnp.cos(ac), jnp.cos(ac)], axis=-1)
    sin = jnp.concatenate([-jnp.sin(ar), jnp.sin(ar), -jnp.sin(ac), jnp.sin(ac)], axis=-1)
    reps = LANES // ROPE_HEAD_DIM
    return jnp.tile(cos, (1, reps)), jnp.tile(sin, (1, reps))


def _tile(total, want):
    t = min(want, total)
    assert total % t == 0
    return t


def kernel(x, c, ctx, c_ctx, mod_w, mod_b, norm1_g, norm2_g, ev_w_in, ev_w_out, sc_conv_w, da_q_norm, da_k_norm,
           da_lq1, da_lk1, da_lq2, da_lk2, da_subln_g, od_w_in, od_w_out, pool_w, pool_scale, ret_decay_f,
           ret_decay_b, ret_gn_g, ffn_w_up, ffn_conv_w, ffn_conv_b, ffn_w_down):
    batch, seq, d = x.shape
    ctx_len = ctx.shape[1]
    depth = mod_w.shape[0]
    t_lat, t_ctx = batch * seq, batch * ctx_len
    xs = x.reshape(t_lat, d)
    cs = ctx.reshape(t_ctx, d)

    pad = (-(batch + 1)) % 8
    c_all = jnp.concatenate([c, c_ctx[None, :], jnp.zeros((pad, d), F32)], axis=0)
    mods = _modulation(c_all, mod_w, mod_b)
    rope = _rope_tables(seq)

    tm_lat = _tile(seq, 1024)
    tm_ctx = _tile(t_ctx, 512)
    tm_ffn_lat = _tile(seq, 1024)
    tm_ffn_ctx = _tile(t_ctx, 512)
    tq = _tile(seq, ATTN_ROWS)
    fdim = ffn_w_down.shape[1]
    fc = 256 if fdim % 256 == 0 else LANES
    ffn_up, ffn_down = ffn_w_up.astype(BF16), ffn_w_down.astype(BF16)

    for i in range(depth):
        last = i == depth - 1
        j = i // 2
        mod_l = mods[i, :batch].reshape(batch, 6, d)
        mod_c = mods[i, batch:batch + 1].reshape(1, 6, d)
        ffn_w = (i, ffn_up, ffn_conv_w, ffn_conv_b, ffn_down)
        if i % 2 == 0:
            w_in = ev_w_in[j].astype(BF16)
            w_out = ev_w_out[j].astype(BF16)
            scw = ev_w_out.shape[1] // 2
            qkw = (w_in.shape[1] - 4 * scw) // 2
            q_scale = ROPE_HEAD_DIM ** -0.5 * LOG2E
            segs = [(3 * scw, 3 * scw + qkw, "qknorm", 0, q_scale),
                    (3 * scw + qkw, 3 * scw + 2 * qkw, "qknorm", 1, 1.0),
                    (0, 3 * scw, "copy", 0, 1.0),
                    (3 * scw + 2 * qkw, w_in.shape[1], "copy", 0, 1.0)]
            dts = [BF16, BF16, BF16, BF16]
            gains = [da_q_norm[j], da_k_norm[j]]
            lam_init = 0.8 - 0.6 * math.exp(-0.3 * i)
            lam_vecs = jnp.stack([da_lq1[j], da_lk1[j], da_lq2[j], da_lk2[j]], axis=0)
            bound = (ROPE_HEAD_DIM * q_scale * BOUND_MARGIN * jnp.max(jnp.abs(da_q_norm[j]))
                     * jnp.max(jnp.abs(da_k_norm[j]))).reshape(1).astype(F32)
            q_l, k_l, g_l, v_l = _in_proj(xs, mod_l, norm1_g[i], w_in, segs, dts, gains, rope, seq, tm_lat,
                                          "in_even_lat")
            q_c, k_c, g_c, v_c = _in_proj(cs, mod_c, norm1_g[i], w_in, segs, dts, gains, None, ctx_len, tm_ctx,
                                          "in_even_ctx")
            o_l = _diff_attn(bound, lam_vecs, q_l, [(k_l, v_l, seq), (k_c, v_c, ctx_len)], seq, tq, 1, lam_init,
                             "attn_lat")
            mix_params = [sc_conv_w[j], da_subln_g[j].reshape(1, HEAD_V)]
            xs = _mixer_ffn("even", xs, [g_l, o_l], mod_l, w_out, mix_params, norm2_g[i], *ffn_w, seq, tm_ffn_lat, fc,
                            1.0 - lam_init, "mix_ffn_even_lat")
            if not last:
                o_c = _diff_attn(bound, lam_vecs, q_c, [(k_c, v_c, ctx_len)], ctx_len, ctx_len, HEADS, lam_init,
                                 "attn_ctx")
                cs = _mixer_ffn("even", cs, [g_c, o_c], mod_c, w_out, mix_params, norm2_g[i], *ffn_w, ctx_len,
                                tm_ffn_ctx, fc, 1.0 - lam_init, "mix_ffn_even_ctx")
        else:
            assert last, "an odd layer that still updates the context stream is not implemented"
            w_in = od_w_in[j].astype(BF16)
            w_out = od_w_out[j].astype(BF16)
            pw = pool_scale.shape[1]
            vw = HEADS * HEAD_V
            qw = (w_in.shape[1] - pw - 2 * vw) // 2
            segs = [(pw, pw + qw, "rope", 0, ROPE_HEAD_DIM ** -0.5),
                    (pw + qw, pw + 2 * qw, "rope", 0, 1.0),
                    (0, pw, "copy", 0, 1.0),
                    (pw + 2 * qw, pw + 2 * qw + vw, "copy", 0, 1.0),
                    (pw + 2 * qw + vw, w_in.shape[1], "copy", 0, 1.0)]
            q_l, k_l, pv, v_l, gate = _in_proj(xs, mod_l, norm1_g[i], w_in, segs, [BF16] * 5, [], rope, seq, tm_lat,
                                               "in_odd_lat")
            w_kv = w_in[:, pw + qw:pw + 2 * qw + vw]
            k_c, v_c = _in_proj(cs, mod_c, norm1_g[i], w_kv, [(0, qw, "copy", 0, 1.0), (qw, qw + vw, "copy", 0, 1.0)],
                                [BF16, BF16], [], None, ctx_len, tm_ctx, "in_odd_ctx")
            o_l = _retention(ret_decay_f[j], ret_decay_b[j], q_l, k_l, v_l, k_c, v_c, seq, ctx_len, "retention")
            mix_params = [pool_w[j].astype(BF16), pool_scale[j].reshape(1, pw), ret_gn_g[j].reshape(1, HEAD_V)]
            xs = _mixer_ffn("odd", xs, [pv, gate, o_l], mod_l, w_out, mix_params, norm2_g[i], *ffn_w, seq, tm_ffn_lat,
                            fc, 1.0, "mix_ffn_odd_lat")
    return xs.reshape(batch, seq, d)
```
